```python
import math
import jax, jax.numpy as jnp
from jax import lax
import numpy as np

D_MODEL = 2048
BATCH = 1
SEQ = 8192
DEPTH = 1

CHUNK = 64
Q_BLOCK = 128
FOX_HEADS = 8
FOX_HEAD_DIM = 128
FOX_WIDTH = FOX_HEADS * FOX_HEAD_DIM
HG_HEADS = 8
HG_KEY_DIM = 128
HG_VAL_DIM = 128
HG_KW = HG_HEADS * HG_KEY_DIM
HG_VW = HG_HEADS * HG_VAL_DIM
PEER_HEADS = 8
PEER_NKEYS = 128
PEER_EXPERTS = PEER_NKEYS * PEER_NKEYS
PEER_DKEY = 256
PEER_HALF = PEER_DKEY // 2
PEER_TOPK = 16
PEER_TOK_BLOCK = 128
EPS = 1e-6
SPLIT_SIZES = (FOX_WIDTH, FOX_WIDTH, FOX_WIDTH, FOX_HEADS,
               HG_KW, HG_KW, HG_VW, HG_VW,
               D_MODEL, D_MODEL)
N_IN = sum(SPLIT_SIZES)

kernel_name = "fox_hgrn2_peer_gated_hybrid"


def rms_norm(x, g):
    xf = x.astype(jnp.float32)
    y = xf * lax.rsqrt(jnp.mean(xf * xf, axis=-1, keepdims=True) + EPS)
    return (y * g.astype(jnp.float32)).astype(x.dtype)


def split_cols(z):
    idx = np.cumsum(np.array(SPLIT_SIZES))[:-1].tolist()
    return jnp.split(z, idx, axis=-1)


def forgetting_attention(q, k, v, log_f):
    B, S, H, Dh = q.shape
    c = jnp.cumsum(log_f, axis=1)
    cT = c.transpose(0, 2, 1)
    nb = S // Q_BLOCK
    qb = q.reshape(B, nb, Q_BLOCK, H, Dh).transpose(1, 0, 2, 3, 4)
    cb = cT.reshape(B, H, nb, Q_BLOCK).transpose(2, 0, 1, 3)
    kpos = jnp.arange(S)
    scale = Dh ** -0.5

    def block(args):
        i, qi, ci = args
        s = jnp.einsum('bqhd,bkhd->bhqk', qi, k,
                       preferred_element_type=jnp.float32) * scale
        bias = ci[..., :, None] - cT[..., None, :]
        qpos = i * Q_BLOCK + jnp.arange(Q_BLOCK)
        mask = kpos[None, :] <= qpos[:, None]
        logits = jnp.where(mask, s + bias, -jnp.inf)
        p = jax.nn.softmax(logits, axis=-1)
        return jnp.einsum('bhqk,bkhd->bqhd', p.astype(v.dtype), v)

    out = lax.map(block, (jnp.arange(nb), qb, cb))
    return out.transpose(1, 0, 2, 3, 4).reshape(B, S, H, Dh)


def hgrn2_recurrence(q, k, v, log_f):
    B, S, H, K = q.shape
    V = v.shape[-1]
    nc = S // CHUNK

    def to_chunks(a):
        return a.reshape(B, nc, CHUNK, H, a.shape[-1]).transpose(1, 0, 3, 2, 4)

    qc, kc, vc, gc = to_chunks(q), to_chunks(k), to_chunks(v), to_chunks(log_f)
    causal = jnp.tril(jnp.ones((CHUNK, CHUNK), dtype=bool))

    def step(state, inp):
        qt, kt, vt, gt = inp
        b = jnp.cumsum(gt, axis=2)
        diff = b[:, :, :, None, :] - b[:, :, None, :, :]
        decay = jnp.exp(jnp.where(causal[None, None, :, :, None], diff, -jnp.inf))
        attn = jnp.einsum('bhtk,bhsk,bhtsk->bhts', qt, kt, decay)
        o = (jnp.einsum('bhts,bhsv->bhtv', attn, vt)
             + jnp.einsum('bhtk,bhkv->bhtv', qt * jnp.exp(b), state))
        b_last = b[:, :, -1:, :]
        new_state = (jnp.exp(b_last[:, :, 0, :])[..., None] * state
                     + jnp.einsum('bhsk,bhsv->bhkv', kt * jnp.exp(b_last - b), vt))
        return new_state, o

    s0 = jnp.zeros((B, H, K, V), jnp.float32)
    _, o = lax.scan(step, s0, (qc, kc, vc, gc))
    return o.transpose(1, 0, 3, 2, 4).reshape(B, S, H, V)


def peer_layer(x, wq, keys, U, Vt):
    B, S, D = x.shape
    T = B * S
    xt = x.reshape(T, D)
    q = (xt @ wq).reshape(T, PEER_HEADS, 2, PEER_HALF)
    scores = jnp.einsum('thcd,hcnd->thcn', q, keys,
                        preferred_element_type=jnp.float32)
    s_top, i_top = lax.top_k(scores, PEER_TOPK)
    cand = s_top[:, :, 0, :, None] + s_top[:, :, 1, None, :]
    cand_idx = i_top[:, :, 0, :, None] * PEER_NKEYS + i_top[:, :, 1, None, :]
    best, pos = lax.top_k(cand.reshape(T, PEER_HEADS, PEER_TOPK * PEER_TOPK), PEER_TOPK)
    idx = jnp.take_along_axis(
        cand_idx.reshape(T, PEER_HEADS, PEER_TOPK * PEER_TOPK), pos, axis=-1)
    gate = jax.nn.softmax(best, axis=-1).astype(x.dtype)

    nb = T // PEER_TOK_BLOCK

    def blk(args):
        xb, ib, gb = args
        u = U[ib]
        hid = jax.nn.gelu(jnp.einsum('td,thkd->thk', xb, u))
        v = Vt[ib]
        return jnp.einsum('thk,thkd->td', gb * hid, v)

    out = lax.map(blk, (xt.reshape(nb, PEER_TOK_BLOCK, D),
                        idx.reshape(nb, PEER_TOK_BLOCK, PEER_HEADS, PEER_TOPK),
                        gate.reshape(nb, PEER_TOK_BLOCK, PEER_HEADS, PEER_TOPK)))
    return out.reshape(B, S, D)


def setup_inputs(seed: int = 0) -> dict:
    key = jax.random.key(seed)
    ks = jax.random.split(key, 16)
    f32 = jnp.float32
    nrm = lambda k, shape, s: (jax.random.normal(k, shape, f32) * s)
    return {
        "x": nrm(ks[0], (BATCH, SEQ, D_MODEL), 1.0),
        "norm1_g": 1.0 + nrm(ks[1], (DEPTH, D_MODEL), 0.02),
        "w_in": nrm(ks[2], (DEPTH, D_MODEL, N_IN), D_MODEL ** -0.5),
        "fox_f_bias": 2.0 + nrm(ks[3], (DEPTH, FOX_HEADS), 0.1),
        "hg_lb_logits": nrm(ks[4], (DEPTH + 1, HG_KW), 0.1),
        "hg_norm_g": 1.0 + nrm(ks[5], (DEPTH, HG_VW), 0.02),
        "w_a_up": nrm(ks[6], (DEPTH, FOX_WIDTH, D_MODEL), FOX_WIDTH ** -0.5),
        "w_b_up": nrm(ks[7], (DEPTH, HG_VW, D_MODEL), HG_VW ** -0.5),
        "w_o": nrm(ks[8], (DEPTH, D_MODEL, D_MODEL), D_MODEL ** -0.5),
        "norm2_g": 1.0 + nrm(ks[9], (DEPTH, D_MODEL), 0.02),
        "peer_wq": nrm(ks[10], (DEPTH, D_MODEL, PEER_HEADS * PEER_DKEY), D_MODEL ** -0.5),
        "peer_keys": nrm(ks[11], (DEPTH, PEER_HEADS, 2, PEER_NKEYS, PEER_HALF), PEER_HALF ** -0.5),
        "peer_u": nrm(ks[12], (DEPTH, PEER_EXPERTS, D_MODEL), D_MODEL ** -0.5),
        "peer_v": nrm(ks[13], (DEPTH, PEER_EXPERTS, D_MODEL), (PEER_HEADS * PEER_TOPK) ** -0.5),
        "norm_f_g": 1.0 + nrm(ks[14], (D_MODEL,), 0.02),
    }


def reference(x, norm1_g, w_in, fox_f_bias, hg_lb_logits, hg_norm_g, w_a_up, w_b_up,
              w_o, norm2_g, peer_wq, peer_keys, peer_u, peer_v, norm_f_g):
    B, S, D = x.shape
    lb_all = jnp.cumsum(jax.nn.softmax(hg_lb_logits.astype(jnp.float32), axis=0), axis=0)
    for l in range(DEPTH):
        h = rms_norm(x, norm1_g[l])
        z = h @ w_in[l]
        fq, fk, fv, ff, hq, hf, hi, hgate, ga, gb = split_cols(z)

        log_fa = jax.nn.log_sigmoid(ff.astype(jnp.float32) + fox_f_bias[l].astype(jnp.float32))
        shp = (B, S, FOX_HEADS, FOX_HEAD_DIM)
        ya = forgetting_attention(fq.reshape(shp), fk.reshape(shp), fv.reshape(shp), log_fa)
        ya = ya.reshape(B, S, FOX_WIDTH) @ w_a_up[l]

        lb = lb_all[l]
        f_gate = lb + (1.0 - lb) * jax.nn.sigmoid(hf.astype(jnp.float32))
        log_fb = jnp.log(f_gate).reshape(B, S, HG_HEADS, HG_KEY_DIM)
        k_in = (1.0 - f_gate).reshape(B, S, HG_HEADS, HG_KEY_DIM)
        q_in = jax.nn.silu(hq.astype(jnp.float32)).reshape(B, S, HG_HEADS, HG_KEY_DIM)
        v_in = hi.astype(jnp.float32).reshape(B, S, HG_HEADS, HG_VAL_DIM)
        ob = hgrn2_recurrence(q_in, k_in, v_in, log_fb)
        ob = ob * lax.rsqrt(jnp.mean(ob * ob, axis=-1, keepdims=True) + EPS)
        ob = ob.reshape(B, S, HG_VW) * hg_norm_g[l].astype(jnp.float32)
        ob = (ob * jax.nn.silu(hgate.astype(jnp.float32))).astype(x.dtype)
        yb = ob @ w_b_up[l]

        merged = jax.nn.sigmoid(ga) * ya + jax.nn.sigmoid(gb) * yb
        x = x + merged @ w_o[l]

        h2 = rms_norm(x, norm2_g[l])
        x = x + peer_layer(h2, peer_wq[l], peer_keys[l], peer_u[l], peer_v[l])
    return rms_norm(x, norm_f_g)
```

```python
import functools
import math

import jax
import jax.numpy as jnp
from jax import lax
from jax.experimental import pallas as pl
from jax.experimental.pallas import tpu as pltpu

F32 = jnp.float32
BF16 = jnp.bfloat16

D_MODEL = 2048
HEADS = 8
HEAD_DIM = 128
PEER_NKEYS = 128
PEER_TOPK = 16
EPS = 1e-6
LANES = 128
HG_CHUNK = 64
HG_SUB = 16
VMEM_LIMIT = 56 * 1024 * 1024
NEG_BIG = -1e30

COL_GA, COL_GB = 0, 16
COL_FQ, COL_FK, COL_FV = 32, 40, 48
COL_HQ, COL_HF, COL_HI, COL_HGATE = 56, 64, 72, 80
N_MAIN = 88 * LANES

NT_DIMS = (((1,), (1,)), ((), ()))


def _split3(a):
    a1 = a.astype(BF16)
    r1 = a - a1.astype(F32)
    a2 = r1.astype(BF16)
    a3 = (r1 - a2.astype(F32)).astype(BF16)
    return a1, a2, a3


def _dot(a, b):
    return jnp.dot(a, b, preferred_element_type=F32)


def _dot_nt(a, b):
    return lax.dot_general(a, b, NT_DIMS, preferred_element_type=F32)


def _sigmoid(a):
    return 1.0 / (1.0 + jnp.exp(-a))


def _inproj_kernel(x_ref, g_ref, w_ref, wff_ref, z_ref, ff_ref, h_scr):
    @pl.when(pl.program_id(1) == 0)
    def _():
        x = x_ref[...]
        ms = jnp.mean(x * x, axis=-1, keepdims=True)
        h = x * lax.rsqrt(ms + EPS) * g_ref[...]
        hb = h.astype(BF16)
        h_scr[...] = hb
        hl = (h - hb.astype(F32)).astype(BF16)
        wf = wff_ref[...]
        wb = wf.astype(BF16)
        wl = (wf - wb.astype(F32)).astype(BF16)
        ff_ref[...] = _dot_nt(wb, hb) + _dot_nt(wb, hl) + _dot_nt(wl, hb)

    z_ref[...] = _dot(h_scr[...], w_ref[...])


def _inproj(x, g, w_main, w_ffT, tm, tn):
    S = x.shape[0]
    return pl.pallas_call(
        _inproj_kernel,
        grid=(S // tm, N_MAIN // tn),
        in_specs=[
            pl.BlockSpec((tm, D_MODEL), lambda i, j: (i, 0)),
            pl.BlockSpec((1, D_MODEL), lambda i, j: (0, 0)),
            pl.BlockSpec((D_MODEL, tn), lambda i, j: (0, j)),
            pl.BlockSpec((HEADS, D_MODEL), lambda i, j: (0, 0)),
        ],
        out_specs=[
            pl.BlockSpec((tm, tn), lambda i, j: (i, j)),
            pl.BlockSpec((HEADS, tm), lambda i, j: (0, i)),
        ],
        out_shape=[
            jax.ShapeDtypeStruct((S, N_MAIN), F32),
            jax.ShapeDtypeStruct((HEADS, S), F32),
        ],
        scratch_shapes=[pltpu.VMEM((tm, D_MODEL), BF16)],
        compiler_params=pltpu.CompilerParams(
            dimension_semantics=("parallel", "arbitrary"),
            vmem_limit_bytes=VMEM_LIMIT),
        name="inproj",
    )(x, g, w_main, w_ffT)


def _fox_gate_kernel(ff_ref, bias_ref, c_ref, *, nb):
    a = ff_ref[...] + bias_ref[...]
    lf = jnp.minimum(a, 0.0) - jnp.log(1.0 + jnp.exp(-jnp.abs(a)))
    R = HEADS * nb
    r = lax.broadcasted_iota(jnp.int32, (LANES, LANES), 0)
    c = lax.broadcasted_iota(jnp.int32, (LANES, LANES), 1)
    upper = jnp.where(r <= c, 1.0, 0.0).astype(BF16)
    ones = jnp.ones((LANES, LANES), BF16)
    rr = lax.broadcasted_iota(jnp.int32, (R, R), 0)
    cc = lax.broadcasted_iota(jnp.int32, (R, R), 1)
    same_head = (rr // nb) == (cc // nb)
    carry_m = jnp.where(same_head & (cc < rr), 1.0, 0.0).astype(BF16)
    l1, l2, l3 = _split3(lf)
    within = _dot(l1, upper) + _dot(l2, upper) + _dot(l3, upper)
    tot = _dot(l1, ones) + _dot(l2, ones) + _dot(l3, ones)
    t1, t2, t3 = _split3(tot)
    carry = _dot(carry_m, t1) + _dot(carry_m, t2) + _dot(carry_m, t3)
    c_ref[...] = within + carry


def _fox_gate(ff_rows, bias_rows, nb):
    R = HEADS * nb
    return pl.pallas_call(
        functools.partial(_fox_gate_kernel, nb=nb),
        out_shape=jax.ShapeDtypeStruct((R, LANES), F32),
        compiler_params=pltpu.CompilerParams(vmem_limit_bytes=VMEM_LIMIT),
        name="fox_gate",
    )(ff_rows, bias_rows)


def _fox_attn_kernel(q_ref, k_ref, v_ref, cq_ref, ck_ref, o_ref, qs, m_s, l_s, acc, *, tq):
    i = pl.program_id(1)
    j = pl.program_id(2)

    @pl.when(j == 0)
    def _():
        qs[...] = (q_ref[...] * (HEAD_DIM ** -0.5)).astype(BF16)
        m_s[...] = jnp.full(m_s.shape, NEG_BIG, F32)
        l_s[...] = jnp.zeros(l_s.shape, F32)
        acc[...] = jnp.zeros(acc.shape, F32)

    @pl.when(j <= i)
    def _():
        s = _dot_nt(qs[...], k_ref[...].astype(BF16))
        s = s + cq_ref[...] - ck_ref[...]
        row = lax.broadcasted_iota(jnp.int32, (tq, tq), 0)
        col = lax.broadcasted_iota(jnp.int32, (tq, tq), 1)
        s = jnp.where((col <= row) | (j < i), s, NEG_BIG)
        m_prev = m_s[...]
        m_new = jnp.maximum(m_prev, jnp.max(s, axis=-1, keepdims=True))
        alpha = jnp.exp(m_prev - m_new)
        p = jnp.exp(s - m_new)
        l_s[...] = alpha * l_s[...] + jnp.sum(p, axis=-1, keepdims=True)
        acc[...] = alpha * acc[...] + _dot(p.astype(BF16), v_ref[...].astype(BF16))
        m_s[...] = m_new

    @pl.when(j == i)
    def _():
        o_ref[...] = (acc[...] / l_s[...]).astype(o_ref.dtype)


def _fox_attn(z, cq, ck, tq):
    S = z.shape[0]
    nq = S // tq
    kv_idx = lambda off: (lambda h, i, j: (jnp.minimum(j, i), off + h))
    return pl.pallas_call(
        functools.partial(_fox_attn_kernel, tq=tq),
        grid=(HEADS, nq, nq),
        in_specs=[
            pl.BlockSpec((tq, HEAD_DIM), lambda h, i, j: (i, COL_FQ + h)),
            pl.BlockSpec((tq, HEAD_DIM), kv_idx(COL_FK)),
            pl.BlockSpec((tq, HEAD_DIM), kv_idx(COL_FV)),
            pl.BlockSpec((None, tq, 1), lambda h, i, j: (h, i, 0)),
            pl.BlockSpec((None, 1, tq), lambda h, i, j: (h, 0, jnp.minimum(j, i))),
        ],
        out_specs=pl.BlockSpec((tq, HEAD_DIM), lambda h, i, j: (i, h)),
        out_shape=jax.ShapeDtypeStruct((S, HEADS * HEAD_DIM), BF16),
        scratch_shapes=[
            pltpu.VMEM((tq, HEAD_DIM), BF16),
            pltpu.VMEM((tq, 1), F32),
            pltpu.VMEM((tq, 1), F32),
            pltpu.VMEM((tq, HEAD_DIM), F32),
        ],
        compiler_params=pltpu.CompilerParams(
            dimension_semantics=("parallel", "parallel", "arbitrary"),
            vmem_limit_bytes=VMEM_LIMIT),
        name="fox_attn",
    )(z, z, z, cq, ck)


def _hgrn2_kernel(hq_ref, hf_ref, hi_ref, hg_ref, lbl_ref, ng_ref, o_ref, st, *, n_chunks):
    @pl.when(pl.program_id(1) == 0)
    def _():
        st[...] = jnp.zeros(st.shape, F32)

    lbl = lbl_ref[...]
    mx = jnp.max(lbl, axis=0, keepdims=True)
    ex = jnp.exp(lbl - mx)
    lb = ex[0:1, :] / jnp.sum(ex, axis=0, keepdims=True)
    ng = ng_ref[...]

    C = HG_CHUNK
    r = lax.broadcasted_iota(jnp.int32, (C, C), 0)
    c = lax.broadcasted_iota(jnp.int32, (C, C), 1)
    lower = jnp.where(c <= r, 1.0, 0.0).astype(BF16)
    sub_r = lax.broadcasted_iota(jnp.int32, (HG_SUB, HG_SUB), 0)
    sub_c = lax.broadcasted_iota(jnp.int32, (HG_SUB, HG_SUB), 1)
    diag_mask = sub_c <= sub_r

    def chunk(ci, carry):
        rows = pl.ds(pl.multiple_of(ci * C, C), C)
        f = lb + (1.0 - lb) * _sigmoid(hf_ref[rows, :])
        g = jnp.log(f)
        kk = 1.0 - f
        hq = hq_ref[rows, :]
        q = hq * _sigmoid(hq)
        v = hi_ref[rows, :]
        vb = v.astype(BF16)
        g1, g2, g3 = _split3(g)
        b = _dot(lower, g1) + _dot(lower, g2) + _dot(lower, g3)
        stv = st[...]
        o_inter = _dot_nt((q * jnp.exp(b)).astype(BF16), stv.astype(BF16))
        outs = []
        for s in range(C // HG_SUB):
            r0 = s * HG_SUB
            n = r0 + HG_SUB
            b0 = b[r0 - 1:r0, :] if s > 0 else jnp.zeros((1, HEAD_DIM), F32)
            qi = q[r0:n, :] * jnp.exp(b[r0:n, :] - b0)
            ki = kk[0:n, :] * jnp.exp(b0 - b[0:n, :])
            a = _dot_nt(qi.astype(BF16), ki.astype(BF16))
            if s > 0:
                a_d = jnp.where(diag_mask, a[:, r0:n], 0.0)
                a = jnp.concatenate([a[:, 0:r0], a_d], axis=1)
            else:
                a = jnp.where(diag_mask, a, 0.0)
            outs.append(_dot(a.astype(BF16), vb[0:n, :]))
        o = o_inter + jnp.concatenate(outs, axis=0)
        bl = b[C - 1:C, :]
        kd = kk * jnp.exp(bl - b)
        st[...] = stv * jnp.exp(bl) + _dot(v.T.astype(BF16), kd.astype(BF16))
        on = o * lax.rsqrt(jnp.mean(o * o, axis=-1, keepdims=True) + EPS) * ng
        hg = hg_ref[rows, :]
        o_ref[rows, :] = (on * (hg * _sigmoid(hg))).astype(o_ref.dtype)
        return carry

    lax.fori_loop(0, n_chunks, chunk, 0)


def _hgrn2(z, lb_logits, norm_g, tc):
    S = z.shape[0]
    blk = lambda off: pl.BlockSpec((tc, HEAD_DIM), lambda h, t: (t, off + h))
    return pl.pallas_call(
        functools.partial(_hgrn2_kernel, n_chunks=tc // HG_CHUNK),
        grid=(HEADS, S // tc),
        in_specs=[
            blk(COL_HQ), blk(COL_HF), blk(COL_HI), blk(COL_HGATE),
            pl.BlockSpec((lb_logits.shape[0], HEAD_DIM), lambda h, t: (0, h)),
            pl.BlockSpec((1, HEAD_DIM), lambda h, t: (0, h)),
        ],
        out_specs=pl.BlockSpec((tc, HEAD_DIM), lambda h, t: (t, h)),
        out_shape=jax.ShapeDtypeStruct((S, HEADS * HEAD_DIM), BF16),
        scratch_shapes=[pltpu.VMEM((HEAD_DIM, HEAD_DIM), F32)],
        compiler_params=pltpu.CompilerParams(
            dimension_semantics=("parallel", "arbitrary"),
            vmem_limit_bytes=VMEM_LIMIT),
        name="hgrn2",
    )(z, z, z, z, lb_logits, norm_g)


def _merge_kernel(ya_ref, ob_ref, ga_ref, gb_ref, x_ref, wa_ref, wb_ref, wo_ref, g2_ref,
                  x1_ref, h2_ref):
    ya = _dot(ya_ref[...], wa_ref[...])
    yb = _dot(ob_ref[...], wb_ref[...])
    merged = _sigmoid(ga_ref[...]) * ya + _sigmoid(gb_ref[...]) * yb
    x1 = x_ref[...] + _dot(merged.astype(BF16), wo_ref[...])
    x1_ref[...] = x1
    ms = jnp.mean(x1 * x1, axis=-1, keepdims=True)
    h2_ref[...] = (x1 * lax.rsqrt(ms + EPS) * g2_ref[...]).astype(h2_ref.dtype)


def _merge(ya, ob, z, x, wa, wb, wo, g2, tm):
    S = x.shape[0]
    W = HEADS * HEAD_DIM
    const = lambda shape: pl.BlockSpec(shape, lambda i: (0, 0), pipeline_mode=pl.Buffered(1))
    return pl.pallas_call(
        _merge_kernel,
        grid=(S // tm,),
        in_specs=[
            pl.BlockSpec((tm, W), lambda i: (i, 0)),
            pl.BlockSpec((tm, W), lambda i: (i, 0)),
            pl.BlockSpec((tm, D_MODEL), lambda i: (i, COL_GA * LANES // D_MODEL)),
            pl.BlockSpec((tm, D_MODEL), lambda i: (i, COL_GB * LANES // D_MODEL)),
            pl.BlockSpec((tm, D_MODEL), lambda i: (i, 0)),
            const((W, D_MODEL)), const((W, D_MODEL)), const((D_MODEL, D_MODEL)),
            const((1, D_MODEL)),
        ],
        out_specs=[
            pl.BlockSpec((tm, D_MODEL), lambda i: (i, 0)),
            pl.BlockSpec((tm, D_MODEL), lambda i: (i, 0)),
        ],
        out_shape=[
            jax.ShapeDtypeStruct((S, D_MODEL), F32),
            jax.ShapeDtypeStruct((S, D_MODEL), BF16),
        ],
        compiler_params=pltpu.CompilerParams(
            dimension_semantics=("parallel",), vmem_limit_bytes=VMEM_LIMIT),
        name="merge",
    )(ya, ob, z, z, x, wa, wb, wo, g2)


def _top_vals(vals, n):
    tops = []
    for _ in range(n):
        m = jnp.max(vals, axis=0, keepdims=True)
        tops.append(m)
        vals = jnp.where(vals == m, NEG_BIG, vals)
    return tops


def _peer_route_kernel(h2t_ref, wqt_ref, keys_ref, s2_ref, e2z_ref, th_ref, e1_ref):
    qT = _dot(wqt_ref[...], h2t_ref[...])
    ntop = PEER_TOPK + 1
    for h in range(HEADS):
        sc = []
        for half in range(2):
            j = 2 * h + half
            qj = qT[j * HEAD_DIM:(j + 1) * HEAD_DIM, :]
            q1, q2, _ = _split3(qj)
            kf = keys_ref[j]
            k1, k2, _ = _split3(kf)
            sc.append(_dot(k1, q1) + _dot(k1, q2) + _dot(k2, q1))
        s1, s2 = sc
        t1 = _top_vals(s1, ntop)
        t2 = _top_vals(s2, ntop)
        cand = [t1[a] + t2[b] for a in range(ntop) for b in range(ntop) if (a + 1) * (b + 1) <= ntop]
        cand = jnp.concatenate(cand, axis=0)
        ctop = _top_vals(cand, ntop)
        tau = 0.5 * (ctop[PEER_TOPK - 1] + ctop[PEER_TOPK])
        cmax = t1[0] + t2[0]
        zsum = jnp.sum(jnp.where(cand >= tau, jnp.exp(cand - cmax), 0.0), axis=0, keepdims=True)
        rows = slice(h * PEER_NKEYS, (h + 1) * PEER_NKEYS)
        s2_ref[rows, :] = s2
        e2z_ref[rows, :] = jnp.exp(s2 - t2[0]) / zsum
        th_ref[:, h, :] = tau - s1
        e1_ref[:, h, :] = jnp.exp(s1 - t1[0])


def _peer_route(h2t, wqt, keys, tb):
    S = h2t.shape[1]
    R = HEADS * PEER_NKEYS
    out = jax.ShapeDtypeStruct((R, S), F32)
    ospec = pl.BlockSpec((R, tb), lambda i: (0, i))
    out_k = jax.ShapeDtypeStruct((PEER_NKEYS, HEADS, S), F32)
    ospec_k = pl.BlockSpec((PEER_NKEYS, HEADS, tb), lambda i: (0, 0, i))
    return pl.pallas_call(
        _peer_route_kernel,
        grid=(S // tb,),
        in_specs=[
            pl.BlockSpec((D_MODEL, tb), lambda i: (0, i)),
            pl.BlockSpec((2 * R, D_MODEL), lambda i: (0, 0), pipeline_mode=pl.Buffered(1)),
            pl.BlockSpec((2 * HEADS, PEER_NKEYS, HEAD_DIM), lambda i: (0, 0, 0)),
        ],
        out_specs=[ospec, ospec, ospec_k, ospec_k],
        out_shape=[out, out, out_k, out_k],
        compiler_params=pltpu.CompilerParams(
            dimension_semantics=("parallel",), vmem_limit_bytes=VMEM_LIMIT),
        name="peer_route",
    )(h2t, wqt, keys)


def _gelu_tanh(a):
    return 0.5 * a * (1.0 + jnp.tanh(math.sqrt(2.0 / math.pi) * (a + 0.044715 * (a * a * a))))


def _peer_dense_kernel(h2t_ref, u_ref, vt_ref, s2_ref, e2z_ref, th_ref, e1_ref, o_ref, y_scr,
                       *, te, tb):
    e = pl.program_id(1)

    @pl.when(e == 0)
    def _():
        o_ref[...] = jnp.zeros(o_ref.shape, F32)

    hid = _dot(u_ref[...], h2t_ref[...])
    for cc in range(te // PEER_NKEYS):
        for tc in range(tb // LANES):
            cols = slice(tc * LANES, (tc + 1) * LANES)
            w = jnp.zeros((PEER_NKEYS, LANES), F32)
            for h in range(HEADS):
                rows = slice(h * PEER_NKEYS, (h + 1) * PEER_NKEYS)
                th_row = th_ref[cc, h:h + 1, cols]
                e1_row = e1_ref[cc, h:h + 1, cols]
                w = w + jnp.where(s2_ref[rows, cols] >= th_row, e2z_ref[rows, cols], 0.0) * e1_row
            act = _gelu_tanh(hid[cc * PEER_NKEYS:(cc + 1) * PEER_NKEYS, cols])
            y_scr[cc * PEER_NKEYS:(cc + 1) * PEER_NKEYS, cols] = (w * act).astype(BF16)
    o_ref[...] += _dot(vt_ref[...], y_scr[...])


def _peer_dense(h2t, u, vt, s2, e2z, th, e1, tb, te):
    S = h2t.shape[1]
    n_exp = u.shape[0]
    R = HEADS * PEER_NKEYS
    rspec = pl.BlockSpec((R, tb), lambda i, e: (0, i))
    kspec = pl.BlockSpec((te // PEER_NKEYS, HEADS, tb), lambda i, e: (e, 0, i))
    return pl.pallas_call(
        functools.partial(_peer_dense_kernel, te=te, tb=tb),
        grid=(S // tb, n_exp // te),
        in_specs=[
            pl.BlockSpec((D_MODEL, tb), lambda i, e: (0, i)),
            pl.BlockSpec((te, D_MODEL), lambda i, e: (e, 0)),
            pl.BlockSpec((D_MODEL, te), lambda i, e: (0, e)),
            rspec, rspec, kspec, kspec,
        ],
        out_specs=pl.BlockSpec((D_MODEL, tb), lambda i, e: (0, i)),
        out_shape=jax.ShapeDtypeStruct((D_MODEL, S), F32),
        scratch_shapes=[pltpu.VMEM((te, tb), BF16)],
        compiler_params=pltpu.CompilerParams(
            dimension_semantics=("parallel", "arbitrary"), vmem_limit_bytes=VMEM_LIMIT),
        name="peer_dense",
    )(h2t, u, vt, s2, e2z, th, e1)


def _final_kernel(x1_ref, pt_ref, g_ref, o_ref):
    xo = x1_ref[...] + pt_ref[...].T
    ms = jnp.mean(xo * xo, axis=-1, keepdims=True)
    o_ref[...] = xo * lax.rsqrt(ms + EPS) * g_ref[...]


def _final(x1, peer_t, g, tm):
    S = x1.shape[0]
    return pl.pallas_call(
        _final_kernel,
        grid=(S // tm,),
        in_specs=[
            pl.BlockSpec((tm, D_MODEL), lambda i: (i, 0)),
            pl.BlockSpec((D_MODEL, tm), lambda i: (0, i)),
            pl.BlockSpec((1, D_MODEL), lambda i: (0, 0)),
        ],
        out_specs=pl.BlockSpec((tm, D_MODEL), lambda i: (i, 0)),
        out_shape=jax.ShapeDtypeStruct((S, D_MODEL), F32),
        compiler_params=pltpu.CompilerParams(
            dimension_semantics=("parallel",), vmem_limit_bytes=VMEM_LIMIT),
        name="final_norm",
    )(x1, peer_t, g)


def _tile(n, pref):
    t = min(pref, n)
    assert n % t == 0, (n, t)
    return t


def kernel(x, norm1_g, w_in, fox_f_bias, hg_lb_logits, hg_norm_g, w_a_up, w_b_up, w_o, norm2_g,
           peer_wq, peer_keys, peer_u, peer_v, norm_f_g):
    B, S, D = x.shape
    assert B == 1 and D == D_MODEL and w_in.shape[0] == 1 and S % LANES == 0
    xs = x[0]
    W = HEADS * HEAD_DIM

    wi = w_in[0]
    o_ff = 3 * W
    o_rest = o_ff + HEADS
    o_ga = o_rest + 4 * W
    w_main = jnp.concatenate(
        [wi[:, o_ga:], wi[:, :o_ff], wi[:, o_rest:o_ga]], axis=1).astype(BF16)
    w_ffT = wi[:, o_ff:o_rest].T
    wa = w_a_up[0].astype(BF16)
    wb = w_b_up[0].astype(BF16)
    wo = w_o[0].astype(BF16)
    wqt = peer_wq[0].T.astype(BF16)
    keys = peer_keys[0].reshape(2 * HEADS, PEER_NKEYS, HEAD_DIM)
    u = peer_u[0].astype(BF16)
    vt = peer_v[0].T.astype(BF16)

    z, ffT = _inproj(xs, norm1_g, w_main, w_ffT, _tile(S, 1024), 512)

    nb = S // LANES
    bias_rows = jnp.broadcast_to(fox_f_bias[0][:, None, None], (HEADS, nb, LANES))
    c_rows = _fox_gate(ffT.reshape(HEADS * nb, LANES), bias_rows.reshape(HEADS * nb, LANES), nb)
    c = c_rows.reshape(HEADS, S)
    ya = _fox_attn(z, c.reshape(HEADS, S, 1), c.reshape(HEADS, 1, S), _tile(S, 512))

    ob = _hgrn2(z, hg_lb_logits, hg_norm_g, _tile(S, 512))

    x1, h2 = _merge(ya, ob, z, xs, wa, wb, wo, norm2_g, _tile(S, 256))

    h2t = h2.T
    s2, e2z, th, e1 = _peer_route(h2t, wqt, keys, _tile(S, 256))
    peer_t = _peer_dense(h2t, u, vt, s2, e2z, th, e1, _tile(S, 512), 256)

    out = _final(x1, peer_t, norm_f_g.reshape(1, D_MODEL), _tile(S, 256))
    return out[None]
```

```python
import functools
import math

import jax
import jax.numpy as jnp
from jax import lax
from jax.experimental import pallas as pl
from jax.experimental.pallas import tpu as pltpu

F32 = jnp.float32
BF16 = jnp.bfloat16

D_MODEL = 2048
HEADS = 8
HEAD_DIM = 128
PEER_NKEYS = 128
PEER_TOPK = 16
EPS = 1e-6
LANES = 128
HG_CHUNK = 64
HG_SUB = 16
VMEM_LIMIT = 56 * 1024 * 1024
NEG_BIG = -1e30
LOG2E = math.log2(math.e)
FOX_BLK = 512
FOX_SKIP_GAP = 130.0

COL_GA, COL_GB = 0, 16
COL_FQ, COL_FK, COL_FV = 32, 40, 48
COL_HQ, COL_HF, COL_HI, COL_HGATE = 56, 64, 72, 80
N_MAIN = 88 * LANES

NT_DIMS = (((1,), (1,)), ((), ()))


def _split3(a):
    a1 = a.astype(BF16)
    r1 = a - a1.astype(F32)
    a2 = r1.astype(BF16)
    a3 = (r1 - a2.astype(F32)).astype(BF16)
    return a1, a2, a3


def _dot(a, b):
    return jnp.dot(a, b, preferred_element_type=F32)


def _dot_nt(a, b):
    return lax.dot_general(a, b, NT_DIMS, preferred_element_type=F32)


def _sigmoid(a):
    return 1.0 / (1.0 + jnp.exp(-a))


def _inproj_kernel(x_ref, g_ref, w_ref, wff_ref, z_ref, ff_ref, h_scr):
    @pl.when(pl.program_id(1) == 0)
    def _():
        x = x_ref[...]
        ms = jnp.mean(x * x, axis=-1, keepdims=True)
        h = x * lax.rsqrt(ms + EPS) * g_ref[...]
        hb = h.astype(BF16)
        h_scr[...] = hb
        hl = (h - hb.astype(F32)).astype(BF16)
        wf = wff_ref[...]
        wb = wf.astype(BF16)
        wl = (wf - wb.astype(F32)).astype(BF16)
        ff_ref[...] = _dot_nt(wb, hb) + _dot_nt(wb, hl) + _dot_nt(wl, hb)

    z_ref[...] = _dot(h_scr[...], w_ref[...]).astype(z_ref.dtype)


def _inproj(x, g, w_main, w_ffT, tm, tn):
    S = x.shape[0]
    return pl.pallas_call(
        _inproj_kernel,
        grid=(S // tm, N_MAIN // tn),
        in_specs=[
            pl.BlockSpec((tm, D_MODEL), lambda i, j: (i, 0)),
            pl.BlockSpec((1, D_MODEL), lambda i, j: (0, 0)),
            pl.BlockSpec((D_MODEL, tn), lambda i, j: (0, j)),
            pl.BlockSpec((HEADS, D_MODEL), lambda i, j: (0, 0)),
        ],
        out_specs=[
            pl.BlockSpec((tm, tn), lambda i, j: (i, j)),
            pl.BlockSpec((HEADS, tm), lambda i, j: (0, i)),
        ],
        out_shape=[
            jax.ShapeDtypeStruct((S, N_MAIN), BF16),
            jax.ShapeDtypeStruct((HEADS, S), F32),
        ],
        scratch_shapes=[pltpu.VMEM((tm, D_MODEL), BF16)],
        compiler_params=pltpu.CompilerParams(
            dimension_semantics=("parallel", "arbitrary"),
            vmem_limit_bytes=VMEM_LIMIT),
        name="inproj",
    )(x, g, w_main, w_ffT)


def _fox_gate_kernel(ff_ref, bias_ref, c_ref, *, nb):
    a = ff_ref[...] + bias_ref[...]
    lf = jnp.minimum(a, 0.0) - jnp.log(1.0 + jnp.exp(-jnp.abs(a)))
    R = HEADS * nb
    r = lax.broadcasted_iota(jnp.int32, (LANES, LANES), 0)
    c = lax.broadcasted_iota(jnp.int32, (LANES, LANES), 1)
    upper = jnp.where(r <= c, 1.0, 0.0).astype(BF16)
    ones = jnp.ones((LANES, LANES), BF16)
    rr = lax.broadcasted_iota(jnp.int32, (R, R), 0)
    cc = lax.broadcasted_iota(jnp.int32, (R, R), 1)
    same_head = (rr // nb) == (cc // nb)
    carry_m = jnp.where(same_head & (cc < rr), 1.0, 0.0).astype(BF16)
    l1, l2, l3 = _split3(lf)
    within = _dot(l1, upper) + _dot(l2, upper) + _dot(l3, upper)
    tot = _dot(l1, ones) + _dot(l2, ones) + _dot(l3, ones)
    t1, t2, t3 = _split3(tot)
    carry = _dot(carry_m, t1) + _dot(carry_m, t2) + _dot(carry_m, t3)
    c_ref[...] = (within + carry) * LOG2E


def _fox_gate(ff_rows, bias_rows, nb):
    R = HEADS * nb
    return pl.pallas_call(
        functools.partial(_fox_gate_kernel, nb=nb),
        out_shape=jax.ShapeDtypeStruct((R, LANES), F32),
        compiler_params=pltpu.CompilerParams(vmem_limit_bytes=VMEM_LIMIT),
        name="fox_gate",
    )(ff_rows, bias_rows)


def _fox_stats_kernel(qk_ref, o_ref, *, nblk):
    x = qk_ref[...].astype(F32)
    x2 = (x * x).astype(BF16)
    r = lax.broadcasted_iota(jnp.int32, (2 * HEADS * HEAD_DIM, LANES), 0)
    c = lax.broadcasted_iota(jnp.int32, (2 * HEADS * HEAD_DIM, LANES), 1)
    grp = jnp.where(r // HEAD_DIM == c, 1.0, 0.0).astype(BF16)
    n2 = _dot(x2, grp)
    o_ref[...] = jnp.max(n2.reshape(nblk, FOX_BLK, LANES), axis=1)


def _fox_stats(z):
    S = z.shape[0]
    rows = min(S, 2 * FOX_BLK)
    assert S % rows == 0 and (COL_FK - COL_FQ) == HEADS and COL_FQ % (2 * HEADS) == 0
    nblk = rows // FOX_BLK
    st = pl.pallas_call(
        functools.partial(_fox_stats_kernel, nblk=nblk),
        grid=(S // rows,),
        in_specs=[pl.BlockSpec((rows, 2 * HEADS * HEAD_DIM), lambda i: (i, COL_FQ // (2 * HEADS)))],
        out_specs=pl.BlockSpec((None, nblk, LANES), lambda i: (i, 0, 0)),
        out_shape=jax.ShapeDtypeStruct((S // rows, nblk, LANES), F32),
        compiler_params=pltpu.CompilerParams(
            dimension_semantics=("parallel",), vmem_limit_bytes=VMEM_LIMIT),
        name="fox_stats",
    )(z)
    return st.reshape(S // FOX_BLK, LANES)


def _fox_attn_kernel(skip_ref, q_ref, k_ref, v_ref, cq_ref, ck_ref, o_ref, m_s, l_s, acc, *, nq):
    h = pl.program_id(0)
    i = pl.program_id(1)
    B = FOX_BLK
    q = (q_ref[...].astype(F32) * (HEAD_DIM ** -0.5 * LOG2E)).astype(BF16)
    cq = cq_ref[...]
    m_s[...] = jnp.full(m_s.shape, NEG_BIG, F32)
    l_s[...] = jnp.zeros(l_s.shape, F32)
    acc[...] = jnp.zeros(acc.shape, F32)

    def update(s, vblk):
        m_prev = m_s[...]
        m_new = jnp.maximum(m_prev, jnp.max(s, axis=-1, keepdims=True))
        alpha = jnp.exp2(m_prev - m_new)
        p = jnp.exp2(s - m_new)
        l_s[...] = alpha * l_s[...] + jnp.sum(p, axis=-1, keepdims=True)
        acc[...] = alpha * acc[...] + _dot(p.astype(BF16), vblk)
        m_s[...] = m_new

    def body(j, carry):
        @pl.when(skip_ref[(h * nq + i) * nq + j] == 0)
        def _():
            rows = pl.ds(pl.multiple_of(j * B, B), B)
            s = _dot_nt(q, k_ref[rows, :]) + cq - ck_ref[j]
            update(s, v_ref[rows, :])
        return carry

    lax.fori_loop(0, i, body, 0)

    rows = pl.ds(pl.multiple_of(i * B, B), B)
    s = _dot_nt(q, k_ref[rows, :]) + cq - ck_ref[i]
    row = lax.broadcasted_iota(jnp.int32, (B, B), 0)
    col = lax.broadcasted_iota(jnp.int32, (B, B), 1)
    update(jnp.where(col <= row, s, NEG_BIG), v_ref[rows, :])
    o_ref[...] = (acc[...] / l_s[...]).astype(o_ref.dtype)


def _fox_attn(z, c2, skip):
    S = z.shape[0]
    B = FOX_BLK
    nq = S // B
    grid_spec = pltpu.PrefetchScalarGridSpec(
        num_scalar_prefetch=1,
        grid=(HEADS, nq),
        in_specs=[
            pl.BlockSpec((B, HEAD_DIM), lambda h, i, sk: (i, COL_FQ + h)),
            pl.BlockSpec((S, HEAD_DIM), lambda h, i, sk: (0, COL_FK + h)),
            pl.BlockSpec((S, HEAD_DIM), lambda h, i, sk: (0, COL_FV + h)),
            pl.BlockSpec((None, B, 1), lambda h, i, sk: (h, i, 0)),
            pl.BlockSpec((None, nq, 1, B), lambda h, i, sk: (h, 0, 0, 0)),
        ],
        out_specs=pl.BlockSpec((B, HEAD_DIM), lambda h, i, sk: (i, h)),
        scratch_shapes=[
            pltpu.VMEM((B, 1), F32),
            pltpu.VMEM((B, 1), F32),
            pltpu.VMEM((B, HEAD_DIM), F32),
        ],
    )
    return pl.pallas_call(
        functools.partial(_fox_attn_kernel, nq=nq),
        grid_spec=grid_spec,
        out_shape=jax.ShapeDtypeStruct((S, HEADS * HEAD_DIM), BF16),
        compiler_params=pltpu.CompilerParams(
            dimension_semantics=("parallel", "arbitrary"),
            vmem_limit_bytes=VMEM_LIMIT),
        name="fox_attn",
    )(skip, z, z, z, c2.reshape(HEADS, S, 1), c2.reshape(HEADS, nq, 1, B))


def _fox_skip_table(stats, c2):
    B = FOX_BLK
    qn = jnp.sqrt(stats[:, 0:HEADS] * 1.02).T
    kn = jnp.sqrt(stats[:, HEADS:2 * HEADS] * 1.02).T
    cf = c2[:, 0::B]
    cl = c2[:, B - 1::B]
    sc = HEAD_DIM ** -0.5 * LOG2E
    bound = (sc * qn[:, :, None] * (kn[:, None, :] + kn[:, :, None])
             + cf[:, :, None] - cl[:, None, :])
    nq = qn.shape[1]
    below = jnp.arange(nq)[None, :] < jnp.arange(nq)[:, None]
    return ((bound < -FOX_SKIP_GAP) & below[None]).astype(jnp.int32).reshape(-1)


def _hgrn2_kernel(hq_ref, hf_ref, hi_ref, hg_ref, lbl_ref, ng_ref, o_ref, st, *, n_chunks):
    @pl.when(pl.program_id(0) == 0)
    def _():
        st[...] = jnp.zeros(st.shape, F32)

    lbl = lbl_ref[...]
    mx = jnp.max(lbl, axis=0, keepdims=True)
    ex = jnp.exp(lbl - mx)
    lb = ex[0:1, :] / jnp.sum(ex, axis=0, keepdims=True)
    ng = ng_ref[...]

    C = HG_CHUNK
    r = lax.broadcasted_iota(jnp.int32, (C, C), 0)
    c = lax.broadcasted_iota(jnp.int32, (C, C), 1)
    causal = c <= r
    lower = jnp.where(causal, 1.0, 0.0).astype(BF16)
    n_sub = C // HG_SUB

    def chunk(ci, carry):
        rows = pl.ds(pl.multiple_of(ci * C, C), C)
        f = lb + (1.0 - lb) * _sigmoid(hf_ref[rows, :].astype(F32))
        g = jnp.log(f)
        kk = 1.0 - f
        hq = hq_ref[rows, :].astype(F32)
        q = hq * _sigmoid(hq)
        vb = hi_ref[rows, :]
        g1, g2, g3 = _split3(g)
        b = _dot(lower, g1) + _dot(lower, g2) + _dot(lower, g3)
        qe = (q * jnp.exp(b)).astype(BF16)
        bl = b[C - 1:C, :]
        kd = (kk * jnp.exp(bl - b)).astype(BF16)
        dec = jnp.exp(bl)
        qis, kis = [], []
        for s in range(n_sub):
            r0 = s * HG_SUB
            b0 = b[r0 - 1:r0, :] if s > 0 else jnp.zeros((1, b.shape[1]), F32)
            qis.append((q[r0:r0 + HG_SUB, :] * jnp.exp(b[r0:r0 + HG_SUB, :] - b0)).astype(BF16))
            kis.append((kk * jnp.exp(b0 - b)).astype(BF16))
        heads = [slice(h * HEAD_DIM, (h + 1) * HEAD_DIM) for h in range(HEADS)]
        stv = [st[h] for h in range(HEADS)]
        o_int = [_dot_nt(qe[:, cs], stv[h].astype(BF16)) for h, cs in enumerate(heads)]
        att = [jnp.concatenate([_dot_nt(qis[s][:, cs], kis[s][:, cs]) for s in range(n_sub)], axis=0)
               for cs in heads]
        st_new = [_dot(vb[:, cs].astype(F32).T.astype(BF16), kd[:, cs]) for cs in heads]
        att = [jnp.where(causal, a, 0.0).astype(BF16) for a in att]
        o_heads = []
        for h, cs in enumerate(heads):
            o = o_int[h] + _dot(att[h], vb[:, cs])
            st[h] = stv[h] * dec[:, cs] + st_new[h]
            o_heads.append(o * lax.rsqrt(jnp.mean(o * o, axis=-1, keepdims=True) + EPS))
        on = jnp.concatenate(o_heads, axis=1) * ng
        hg = hg_ref[rows, :].astype(F32)
        o_ref[rows, :] = (on * (hg * _sigmoid(hg))).astype(o_ref.dtype)
        return carry

    lax.fori_loop(0, n_chunks, chunk, 0)


def _hgrn2(z, lb_logits, norm_g, tc):
    S = z.shape[0]
    W = HEADS * HEAD_DIM
    blk = lambda off: pl.BlockSpec((tc, W), lambda t: (t, off // HEADS))
    assert all(o % HEADS == 0 for o in (COL_HQ, COL_HF, COL_HI, COL_HGATE))
    return pl.pallas_call(
        functools.partial(_hgrn2_kernel, n_chunks=tc // HG_CHUNK),
        grid=(S // tc,),
        in_specs=[
            blk(COL_HQ), blk(COL_HF), blk(COL_HI), blk(COL_HGATE),
            pl.BlockSpec((lb_logits.shape[0], W), lambda t: (0, 0)),
            pl.BlockSpec((1, W), lambda t: (0, 0)),
        ],
        out_specs=pl.BlockSpec((tc, W), lambda t: (t, 0)),
        out_shape=jax.ShapeDtypeStruct((S, W), BF16),
        scratch_shapes=[pltpu.VMEM((HEADS, HEAD_DIM, HEAD_DIM), F32)],
        compiler_params=pltpu.CompilerParams(
            dimension_semantics=("arbitrary",),
            vmem_limit_bytes=VMEM_LIMIT),
        name="hgrn2",
    )(z, z, z, z, lb_logits, norm_g)


def _merge_kernel(ya_ref, ob_ref, ga_ref, gb_ref, x_ref, wa_ref, wb_ref, wo_ref, g2_ref,
                  x1_ref, h2_ref):
    ya = _dot(ya_ref[...], wa_ref[...])
    yb = _dot(ob_ref[...], wb_ref[...])
    merged = _sigmoid(ga_ref[...].astype(F32)) * ya + _sigmoid(gb_ref[...].astype(F32)) * yb
    x1 = x_ref[...] + _dot(merged.astype(BF16), wo_ref[...])
    x1_ref[...] = x1
    ms = jnp.mean(x1 * x1, axis=-1, keepdims=True)
    h2_ref[...] = (x1 * lax.rsqrt(ms + EPS) * g2_ref[...]).astype(h2_ref.dtype)


def _merge(ya, ob, z, x, wa, wb, wo, g2, tm):
    S = x.shape[0]
    W = HEADS * HEAD_DIM
    const = lambda shape: pl.BlockSpec(shape, lambda i: (0, 0), pipeline_mode=pl.Buffered(1))
    return pl.pallas_call(
        _merge_kernel,
        grid=(S // tm,),
        in_specs=[
            pl.BlockSpec((tm, W), lambda i: (i, 0)),
            pl.BlockSpec((tm, W), lambda i: (i, 0)),
            pl.BlockSpec((tm, D_MODEL), lambda i: (i, COL_GA * LANES // D_MODEL)),
            pl.BlockSpec((tm, D_MODEL), lambda i: (i, COL_GB * LANES // D_MODEL)),
            pl.BlockSpec((tm, D_MODEL), lambda i: (i, 0)),
            const((W, D_MODEL)), const((W, D_MODEL)), const((D_MODEL, D_MODEL)),
            const((1, D_MODEL)),
        ],
        out_specs=[
            pl.BlockSpec((tm, D_MODEL), lambda i: (i, 0)),
            pl.BlockSpec((tm, D_MODEL), lambda i: (i, 0)),
        ],
        out_shape=[
            jax.ShapeDtypeStruct((S, D_MODEL), F32),
            jax.ShapeDtypeStruct((S, D_MODEL), BF16),
        ],
        compiler_params=pltpu.CompilerParams(
            dimension_semantics=("parallel",), vmem_limit_bytes=VMEM_LIMIT),
        name="merge",
    )(ya, ob, z, z, x, wa, wb, wo, g2)


def _top_vals(vals, n):
    tops = []
    for _ in range(n):
        m = jnp.max(vals, axis=0, keepdims=True)
        tops.append(m)
        vals = jnp.where(vals == m, NEG_BIG, vals)
    return tops


def _peer_route_kernel(h2t_ref, wqt_ref, keys_ref, s2_ref, e2z_ref, th_ref, e1_ref):
    qT = _dot(wqt_ref[...], h2t_ref[...])
    ntop = PEER_TOPK + 1
    for h in range(HEADS):
        sc = []
        for half in range(2):
            j = 2 * h + half
            qj = qT[j * HEAD_DIM:(j + 1) * HEAD_DIM, :]
            q1, q2, _ = _split3(qj)
            kf = keys_ref[j]
            k1, k2, _ = _split3(kf)
            sc.append(_dot(k1, q1) + _dot(k1, q2) + _dot(k2, q1))
        s1, s2 = sc
        t1 = _top_vals(s1, ntop)
        t2 = _top_vals(s2, ntop)
        cand = [t1[a] + t2[b] for a in range(ntop) for b in range(ntop) if (a + 1) * (b + 1) <= ntop]
        cand = jnp.concatenate(cand, axis=0)
        ctop = _top_vals(cand, ntop)
        tau = 0.5 * (ctop[PEER_TOPK - 1] + ctop[PEER_TOPK])
        cmax = t1[0] + t2[0]
        zsum = jnp.sum(jnp.where(cand >= tau, jnp.exp(cand - cmax), 0.0), axis=0, keepdims=True)
        rows = slice(h * PEER_NKEYS, (h + 1) * PEER_NKEYS)
        s2_ref[rows, :] = s2
        e2z_ref[rows, :] = jnp.exp(s2 - t2[0]) / zsum
        th_ref[:, h, :] = tau - s1
        e1_ref[:, h, :] = jnp.exp(s1 - t1[0])


def _peer_route(h2t, wqt, keys, tb):
    S = h2t.shape[1]
    R = HEADS * PEER_NKEYS
    out = jax.ShapeDtypeStruct((R, S), F32)
    ospec = pl.BlockSpec((R, tb), lambda i: (0, i))
    out_k = jax.ShapeDtypeStruct((PEER_NKEYS, HEADS, S), F32)
    ospec_k = pl.BlockSpec((PEER_NKEYS, HEADS, tb), lambda i: (0, 0, i))
    return pl.pallas_call(
        _peer_route_kernel,
        grid=(S // tb,),
        in_specs=[
            pl.BlockSpec((D_MODEL, tb), lambda i: (0, i)),
            pl.BlockSpec((2 * R, D_MODEL), lambda i: (0, 0), pipeline_mode=pl.Buffered(1)),
            pl.BlockSpec((2 * HEADS, PEER_NKEYS, HEAD_DIM), lambda i: (0, 0, 0)),
        ],
        out_specs=[ospec, ospec, ospec_k, ospec_k],
        out_shape=[out, out, out_k, out_k],
        compiler_params=pltpu.CompilerParams(
            dimension_semantics=("parallel",), vmem_limit_bytes=VMEM_LIMIT),
        name="peer_route",
    )(h2t, wqt, keys)


def _gelu_tanh(a):
    return 0.5 * a * (1.0 + jnp.tanh(math.sqrt(2.0 / math.pi) * (a + 0.044715 * (a * a * a))))


def _peer_dense_kernel(h2t_ref, u_ref, vt_ref, s2_ref, e2z_ref, th_ref, e1_ref, o_ref, y_scr,
                       *, te, tb):
    e = pl.program_id(1)

    @pl.when(e == 0)
    def _():
        o_ref[...] = jnp.zeros(o_ref.shape, F32)

    hid = _dot(u_ref[...], h2t_ref[...])
    for cc in range(te // PEER_NKEYS):
        for tc in range(tb // LANES):
            cols = slice(tc * LANES, (tc + 1) * LANES)
            w = jnp.zeros((PEER_NKEYS, LANES), F32)
            for h in range(HEADS):
                rows = slice(h * PEER_NKEYS, (h + 1) * PEER_NKEYS)
                th_row = th_ref[cc, h:h + 1, cols]
                e1_row = e1_ref[cc, h:h + 1, cols]
                w = w + jnp.where(s2_ref[rows, cols] >= th_row, e2z_ref[rows, cols], 0.0) * e1_row
            act = _gelu_tanh(hid[cc * PEER_NKEYS:(cc + 1) * PEER_NKEYS, cols])
            y_scr[cc * PEER_NKEYS:(cc + 1) * PEER_NKEYS, cols] = (w * act).astype(BF16)
    o_ref[...] += _dot(vt_ref[...], y_scr[...])


def _peer_dense(h2t, u, vt, s2, e2z, th, e1, tb, te):
    S = h2t.shape[1]
    n_exp = u.shape[0]
    R = HEADS * PEER_NKEYS
    rspec = pl.BlockSpec((R, tb), lambda i, e: (0, i))
    kspec = pl.BlockSpec((te // PEER_NKEYS, HEADS, tb), lambda i, e: (e, 0, i))
    return pl.pallas_call(
        functools.partial(_peer_dense_kernel, te=te, tb=tb),
        grid=(S // tb, n_exp // te),
        in_specs=[
            pl.BlockSpec((D_MODEL, tb), lambda i, e: (0, i)),
            pl.BlockSpec((te, D_MODEL), lambda i, e: (e, 0)),
            pl.BlockSpec((D_MODEL, te), lambda i, e: (0, e)),
            rspec, rspec, kspec, kspec,
        ],
        out_specs=pl.BlockSpec((D_MODEL, tb), lambda i, e: (0, i)),
        out_shape=jax.ShapeDtypeStruct((D_MODEL, S), F32),
        scratch_shapes=[pltpu.VMEM((te, tb), BF16)],
        compiler_params=pltpu.CompilerParams(
            dimension_semantics=("parallel", "arbitrary"), vmem_limit_bytes=VMEM_LIMIT),
        name="peer_dense",
    )(h2t, u, vt, s2, e2z, th, e1)


def _final_kernel(x1_ref, pt_ref, g_ref, o_ref):
    xo = x1_ref[...] + pt_ref[...].T
    ms = jnp.mean(xo * xo, axis=-1, keepdims=True)
    o_ref[...] = xo * lax.rsqrt(ms + EPS) * g_ref[...]


def _final(x1, peer_t, g, tm):
    S = x1.shape[0]
    return pl.pallas_call(
        _final_kernel,
        grid=(S // tm,),
        in_specs=[
            pl.BlockSpec((tm, D_MODEL), lambda i: (i, 0)),
            pl.BlockSpec((D_MODEL, tm), lambda i: (0, i)),
            pl.BlockSpec((1, D_MODEL), lambda i: (0, 0)),
        ],
        out_specs=pl.BlockSpec((tm, D_MODEL), lambda i: (i, 0)),
        out_shape=jax.ShapeDtypeStruct((S, D_MODEL), F32),
        compiler_params=pltpu.CompilerParams(
            dimension_semantics=("parallel",), vmem_limit_bytes=VMEM_LIMIT),
        name="final_norm",
    )(x1, peer_t, g)


def _tile(n, pref):
    t = min(pref, n)
    assert n % t == 0, (n, t)
    return t


def kernel(x, norm1_g, w_in, fox_f_bias, hg_lb_logits, hg_norm_g, w_a_up, w_b_up, w_o, norm2_g,
           peer_wq, peer_keys, peer_u, peer_v, norm_f_g):
    B, S, D = x.shape
    assert B == 1 and D == D_MODEL and w_in.shape[0] == 1 and S % LANES == 0
    xs = x[0]
    W = HEADS * HEAD_DIM

    wi = w_in[0]
    o_ff = 3 * W
    o_rest = o_ff + HEADS
    o_ga = o_rest + 4 * W
    w_main = jnp.concatenate(
        [wi[:, o_ga:], wi[:, :o_ff], wi[:, o_rest:o_ga]], axis=1).astype(BF16)
    w_ffT = wi[:, o_ff:o_rest].T
    wa = w_a_up[0].astype(BF16)
    wb = w_b_up[0].astype(BF16)
    wo = w_o[0].astype(BF16)
    wqt = peer_wq[0].T.astype(BF16)
    keys = peer_keys[0].reshape(2 * HEADS, PEER_NKEYS, HEAD_DIM)
    u = peer_u[0].astype(BF16)
    vt = peer_v[0].T.astype(BF16)

    z, ffT = _inproj(xs, norm1_g, w_main, w_ffT, _tile(S, 1024), 512)

    nb = S // LANES
    bias_rows = jnp.broadcast_to(fox_f_bias[0][:, None, None], (HEADS, nb, LANES))
    c_rows = _fox_gate(ffT.reshape(HEADS * nb, LANES), bias_rows.reshape(HEADS * nb, LANES), nb)
    c2 = c_rows.reshape(HEADS, S)
    ya = _fox_attn(z, c2, _fox_skip_table(_fox_stats(z), c2))

    ob = _hgrn2(z, hg_lb_logits, hg_norm_g, _tile(S, 256))

    x1, h2 = _merge(ya, ob, z, xs, wa, wb, wo, norm2_g, _tile(S, 256))

    h2t = h2.T
    s2, e2z, th, e1 = _peer_route(h2t, wqt, keys, _tile(S, 256))
    peer_t = _peer_dense(h2t, u, vt, s2, e2z, th, e1, _tile(S, 512), 256)

    out = _final(x1, peer_t, norm_f_g.reshape(1, D_MODEL), _tile(S, 256))
    return out[None]
```

```python
import functools
import math

import jax
import jax.numpy as jnp
from jax import lax
from jax.experimental import pallas as pl
from jax.experimental.pallas import tpu as pltpu

F32 = jnp.float32
BF16 = jnp.bfloat16

D_MODEL = 2048
HEADS = 8
HEAD_DIM = 128
PEER_NKEYS = 128
PEER_TOPK = 16
EPS = 1e-6
LANES = 128
HG_CHUNK = 64
HG_SUB = 16
VMEM_LIMIT = 56 * 1024 * 1024
NEG_BIG = -1e30
LOG2E = math.log2(math.e)
FOX_BLK = 512
FOX_SKIP_GAP = 130.0

COL_GA, COL_GB = 0, 16
COL_FQ, COL_FK, COL_FV = 32, 40, 48
COL_HQ, COL_HF, COL_HI, COL_HGATE = 56, 64, 72, 80
N_MAIN = 88 * LANES

NT_DIMS = (((1,), (1,)), ((), ()))


def _split3(a):
    a1 = a.astype(BF16)
    r1 = a - a1.astype(F32)
    a2 = r1.astype(BF16)
    a3 = (r1 - a2.astype(F32)).astype(BF16)
    return a1, a2, a3


def _dot(a, b):
    return jnp.dot(a, b, preferred_element_type=F32)


def _dot_nt(a, b):
    return lax.dot_general(a, b, NT_DIMS, preferred_element_type=F32)


def _sigmoid(a):
    return 1.0 / (1.0 + jnp.exp(-a))


def _inproj_kernel(x_ref, g_ref, w_ref, wff_ref, z_ref, ff_ref, h_scr):
    @pl.when(pl.program_id(1) == 0)
    def _():
        x = x_ref[...]
        ms = jnp.mean(x * x, axis=-1, keepdims=True)
        h = x * lax.rsqrt(ms + EPS) * g_ref[...]
        hb = h.astype(BF16)
        h_scr[...] = hb
        hl = (h - hb.astype(F32)).astype(BF16)
        wf = wff_ref[...]
        wb = wf.astype(BF16)
        wl = (wf - wb.astype(F32)).astype(BF16)
        ff_ref[...] = _dot_nt(wb, hb) + _dot_nt(wb, hl) + _dot_nt(wl, hb)

    z_ref[...] = _dot(h_scr[...], w_ref[...]).astype(z_ref.dtype)


def _inproj(x, g, w_main, w_ffT, tm, tn):
    S = x.shape[0]
    return pl.pallas_call(
        _inproj_kernel,
        grid=(S // tm, N_MAIN // tn),
        in_specs=[
            pl.BlockSpec((tm, D_MODEL), lambda i, j: (i, 0)),
            pl.BlockSpec((1, D_MODEL), lambda i, j: (0, 0)),
            pl.BlockSpec((D_MODEL, tn), lambda i, j: (0, j)),
            pl.BlockSpec((HEADS, D_MODEL), lambda i, j: (0, 0)),
        ],
        out_specs=[
            pl.BlockSpec((tm, tn), lambda i, j: (i, j)),
            pl.BlockSpec((HEADS, tm), lambda i, j: (0, i)),
        ],
        out_shape=[
            jax.ShapeDtypeStruct((S, N_MAIN), BF16),
            jax.ShapeDtypeStruct((HEADS, S), F32),
        ],
        scratch_shapes=[pltpu.VMEM((tm, D_MODEL), BF16)],
        compiler_params=pltpu.CompilerParams(
            dimension_semantics=("parallel", "arbitrary"),
            vmem_limit_bytes=VMEM_LIMIT),
        name="inproj",
    )(x, g, w_main, w_ffT)


def _fox_gate_kernel(ff_ref, bias_ref, c_ref, *, nb):
    a = ff_ref[...] + bias_ref[...]
    lf = jnp.minimum(a, 0.0) - jnp.log(1.0 + jnp.exp(-jnp.abs(a)))
    R = HEADS * nb
    r = lax.broadcasted_iota(jnp.int32, (LANES, LANES), 0)
    c = lax.broadcasted_iota(jnp.int32, (LANES, LANES), 1)
    upper = jnp.where(r <= c, 1.0, 0.0).astype(BF16)
    ones = jnp.ones((LANES, LANES), BF16)
    rr = lax.broadcasted_iota(jnp.int32, (R, R), 0)
    cc = lax.broadcasted_iota(jnp.int32, (R, R), 1)
    same_head = (rr // nb) == (cc // nb)
    carry_m = jnp.where(same_head & (cc < rr), 1.0, 0.0).astype(BF16)
    l1, l2, l3 = _split3(lf)
    within = _dot(l1, upper) + _dot(l2, upper) + _dot(l3, upper)
    tot = _dot(l1, ones) + _dot(l2, ones) + _dot(l3, ones)
    t1, t2, t3 = _split3(tot)
    carry = _dot(carry_m, t1) + _dot(carry_m, t2) + _dot(carry_m, t3)
    c_ref[...] = (within + carry) * LOG2E


def _fox_gate(ff_rows, bias_rows, nb):
    R = HEADS * nb
    return pl.pallas_call(
        functools.partial(_fox_gate_kernel, nb=nb),
        out_shape=jax.ShapeDtypeStruct((R, LANES), F32),
        compiler_params=pltpu.CompilerParams(vmem_limit_bytes=VMEM_LIMIT),
        name="fox_gate",
    )(ff_rows, bias_rows)


def _fox_stats_kernel(qk_ref, o_ref, *, nblk):
    x = qk_ref[...].astype(F32)
    x2 = (x * x).astype(BF16)
    r = lax.broadcasted_iota(jnp.int32, (2 * HEADS * HEAD_DIM, LANES), 0)
    c = lax.broadcasted_iota(jnp.int32, (2 * HEADS * HEAD_DIM, LANES), 1)
    grp = jnp.where(r // HEAD_DIM == c, 1.0, 0.0).astype(BF16)
    n2 = _dot(x2, grp)
    o_ref[...] = jnp.max(n2.reshape(nblk, FOX_BLK, LANES), axis=1)


def _fox_stats(z):
    S = z.shape[0]
    rows = min(S, 2 * FOX_BLK)
    assert S % rows == 0 and (COL_FK - COL_FQ) == HEADS and COL_FQ % (2 * HEADS) == 0
    nblk = rows // FOX_BLK
    st = pl.pallas_call(
        functools.partial(_fox_stats_kernel, nblk=nblk),
        grid=(S // rows,),
        in_specs=[pl.BlockSpec((rows, 2 * HEADS * HEAD_DIM), lambda i: (i, COL_FQ // (2 * HEADS)))],
        out_specs=pl.BlockSpec((None, nblk, LANES), lambda i: (i, 0, 0)),
        out_shape=jax.ShapeDtypeStruct((S // rows, nblk, LANES), F32),
        compiler_params=pltpu.CompilerParams(
            dimension_semantics=("parallel",), vmem_limit_bytes=VMEM_LIMIT),
        name="fox_stats",
    )(z)
    return st.reshape(S // FOX_BLK, LANES)


def _fox_attn_kernel(skip_ref, q_ref, k_ref, v_ref, cq_ref, ck_ref, o_ref, m_s, l_s, acc, *, nq):
    h = pl.program_id(0)
    i = pl.program_id(1)
    B = FOX_BLK
    q = (q_ref[...].astype(F32) * (HEAD_DIM ** -0.5 * LOG2E)).astype(BF16)
    cq = cq_ref[...]
    m_s[...] = jnp.full(m_s.shape, NEG_BIG, F32)
    l_s[...] = jnp.zeros(l_s.shape, F32)
    acc[...] = jnp.zeros(acc.shape, F32)

    def update(s, vblk):
        m_prev = m_s[...]
        m_new = jnp.maximum(m_prev, jnp.max(s, axis=-1, keepdims=True))
        alpha = jnp.exp2(m_prev - m_new)
        p = jnp.exp2(s - m_new)
        l_s[...] = alpha * l_s[...] + jnp.sum(p, axis=-1, keepdims=True)
        acc[...] = alpha * acc[...] + _dot(p.astype(BF16), vblk)
        m_s[...] = m_new

    def body(j, carry):
        @pl.when(skip_ref[(h * nq + i) * nq + j] == 0)
        def _():
            rows = pl.ds(pl.multiple_of(j * B, B), B)
            s = _dot_nt(q, k_ref[rows, :]) + cq - ck_ref[j]
            update(s, v_ref[rows, :])
        return carry

    lax.fori_loop(0, i, body, 0)

    rows = pl.ds(pl.multiple_of(i * B, B), B)
    s = _dot_nt(q, k_ref[rows, :]) + cq - ck_ref[i]
    row = lax.broadcasted_iota(jnp.int32, (B, B), 0)
    col = lax.broadcasted_iota(jnp.int32, (B, B), 1)
    update(jnp.where(col <= row, s, NEG_BIG), v_ref[rows, :])
    o_ref[...] = (acc[...] / l_s[...]).astype(o_ref.dtype)


def _fox_attn(z, c2, skip):
    S = z.shape[0]
    B = FOX_BLK
    nq = S // B
    grid_spec = pltpu.PrefetchScalarGridSpec(
        num_scalar_prefetch=1,
        grid=(HEADS, nq),
        in_specs=[
            pl.BlockSpec((B, HEAD_DIM), lambda h, i, sk: (i, COL_FQ + h)),
            pl.BlockSpec((S, HEAD_DIM), lambda h, i, sk: (0, COL_FK + h)),
            pl.BlockSpec((S, HEAD_DIM), lambda h, i, sk: (0, COL_FV + h)),
            pl.BlockSpec((None, B, 1), lambda h, i, sk: (h, i, 0)),
            pl.BlockSpec((None, nq, 1, B), lambda h, i, sk: (h, 0, 0, 0)),
        ],
        out_specs=pl.BlockSpec((B, HEAD_DIM), lambda h, i, sk: (i, h)),
        scratch_shapes=[
            pltpu.VMEM((B, 1), F32),
            pltpu.VMEM((B, 1), F32),
            pltpu.VMEM((B, HEAD_DIM), F32),
        ],
    )
    return pl.pallas_call(
        functools.partial(_fox_attn_kernel, nq=nq),
        grid_spec=grid_spec,
        out_shape=jax.ShapeDtypeStruct((S, HEADS * HEAD_DIM), BF16),
        compiler_params=pltpu.CompilerParams(
            dimension_semantics=("parallel", "arbitrary"),
            vmem_limit_bytes=VMEM_LIMIT),
        name="fox_attn",
    )(skip, z, z, z, c2.reshape(HEADS, S, 1), c2.reshape(HEADS, nq, 1, B))


def _fox_skip_table(stats, c2):
    B = FOX_BLK
    qn = jnp.sqrt(stats[:, 0:HEADS] * 1.02).T
    kn = jnp.sqrt(stats[:, HEADS:2 * HEADS] * 1.02).T
    cf = c2[:, 0::B]
    cl = c2[:, B - 1::B]
    sc = HEAD_DIM ** -0.5 * LOG2E
    bound = (sc * qn[:, :, None] * (kn[:, None, :] + kn[:, :, None])
             + cf[:, :, None] - cl[:, None, :])
    nq = qn.shape[1]
    below = jnp.arange(nq)[None, :] < jnp.arange(nq)[:, None]
    return ((bound < -FOX_SKIP_GAP) & below[None]).astype(jnp.int32).reshape(-1)


def _hgrn2_kernel(hq_ref, hf_ref, hi_ref, hg_ref, lbl_ref, ng_ref, o_ref, st, *, n_chunks):
    @pl.when(pl.program_id(0) == 0)
    def _():
        st[...] = jnp.zeros(st.shape, F32)

    lbl = lbl_ref[...]
    mx = jnp.max(lbl, axis=0, keepdims=True)
    ex = jnp.exp(lbl - mx)
    lb = ex[0:1, :] / jnp.sum(ex, axis=0, keepdims=True)
    ng = ng_ref[...]

    C = HG_CHUNK
    r = lax.broadcasted_iota(jnp.int32, (C, C), 0)
    c = lax.broadcasted_iota(jnp.int32, (C, C), 1)
    causal = c <= r
    lower = jnp.where(causal, 1.0, 0.0).astype(BF16)
    n_sub = C // HG_SUB

    def chunk(ci, carry):
        rows = pl.ds(pl.multiple_of(ci * C, C), C)
        f = lb + (1.0 - lb) * _sigmoid(hf_ref[rows, :].astype(F32))
        g = jnp.log(f)
        kk = 1.0 - f
        hq = hq_ref[rows, :].astype(F32)
        q = hq * _sigmoid(hq)
        vb = hi_ref[rows, :]
        g1, g2, g3 = _split3(g)
        b = _dot(lower, g1) + _dot(lower, g2) + _dot(lower, g3)
        qe = (q * jnp.exp(b)).astype(BF16)
        bl = b[C - 1:C, :]
        kd = (kk * jnp.exp(bl - b)).astype(BF16)
        dec = jnp.exp(bl)
        qis, kis = [], []
        for s in range(n_sub):
            r0 = s * HG_SUB
            b0 = b[r0 - 1:r0, :] if s > 0 else jnp.zeros((1, b.shape[1]), F32)
            qis.append((q[r0:r0 + HG_SUB, :] * jnp.exp(b[r0:r0 + HG_SUB, :] - b0)).astype(BF16))
            kis.append((kk * jnp.exp(b0 - b)).astype(BF16))
        heads = [slice(h * HEAD_DIM, (h + 1) * HEAD_DIM) for h in range(HEADS)]
        stv = [st[h] for h in range(HEADS)]
        o_int = [_dot_nt(qe[:, cs], stv[h].astype(BF16)) for h, cs in enumerate(heads)]
        att = [jnp.concatenate([_dot_nt(qis[s][:, cs], kis[s][:, cs]) for s in range(n_sub)], axis=0)
               for cs in heads]
        st_new = [_dot(vb[:, cs].astype(F32).T.astype(BF16), kd[:, cs]) for cs in heads]
        att = [jnp.where(causal, a, 0.0).astype(BF16) for a in att]
        o_heads = []
        for h, cs in enumerate(heads):
            o = o_int[h] + _dot(att[h], vb[:, cs])
            st[h] = stv[h] * dec[:, cs] + st_new[h]
            o_heads.append(o * lax.rsqrt(jnp.mean(o * o, axis=-1, keepdims=True) + EPS))
        on = jnp.concatenate(o_heads, axis=1) * ng
        hg = hg_ref[rows, :].astype(F32)
        o_ref[rows, :] = (on * (hg * _sigmoid(hg))).astype(o_ref.dtype)
        return carry

    lax.fori_loop(0, n_chunks, chunk, 0)


def _hgrn2(z, lb_logits, norm_g, tc):
    S = z.shape[0]
    W = HEADS * HEAD_DIM
    blk = lambda off: pl.BlockSpec((tc, W), lambda t: (t, off // HEADS))
    assert all(o % HEADS == 0 for o in (COL_HQ, COL_HF, COL_HI, COL_HGATE))
    return pl.pallas_call(
        functools.partial(_hgrn2_kernel, n_chunks=tc // HG_CHUNK),
        grid=(S // tc,),
        in_specs=[
            blk(COL_HQ), blk(COL_HF), blk(COL_HI), blk(COL_HGATE),
            pl.BlockSpec((lb_logits.shape[0], W), lambda t: (0, 0)),
            pl.BlockSpec((1, W), lambda t: (0, 0)),
        ],
        out_specs=pl.BlockSpec((tc, W), lambda t: (t, 0)),
        out_shape=jax.ShapeDtypeStruct((S, W), BF16),
        scratch_shapes=[pltpu.VMEM((HEADS, HEAD_DIM, HEAD_DIM), F32)],
        compiler_params=pltpu.CompilerParams(
            dimension_semantics=("arbitrary",),
            vmem_limit_bytes=VMEM_LIMIT),
        name="hgrn2",
    )(z, z, z, z, lb_logits, norm_g)


def _merge_kernel(ya_ref, ob_ref, ga_ref, gb_ref, x_ref, wa_ref, wb_ref, wo_ref, g2_ref,
                  x1_ref, h2_ref):
    ya = _dot(ya_ref[...], wa_ref[...])
    yb = _dot(ob_ref[...], wb_ref[...])
    merged = _sigmoid(ga_ref[...].astype(F32)) * ya + _sigmoid(gb_ref[...].astype(F32)) * yb
    x1 = x_ref[...] + _dot(merged.astype(BF16), wo_ref[...])
    x1_ref[...] = x1
    ms = jnp.mean(x1 * x1, axis=-1, keepdims=True)
    h2_ref[...] = (x1 * lax.rsqrt(ms + EPS) * g2_ref[...]).astype(h2_ref.dtype)


def _merge(ya, ob, z, x, wa, wb, wo, g2, tm):
    S = x.shape[0]
    W = HEADS * HEAD_DIM
    const = lambda shape: pl.BlockSpec(shape, lambda i: (0, 0), pipeline_mode=pl.Buffered(1))
    return pl.pallas_call(
        _merge_kernel,
        grid=(S // tm,),
        in_specs=[
            pl.BlockSpec((tm, W), lambda i: (i, 0)),
            pl.BlockSpec((tm, W), lambda i: (i, 0)),
            pl.BlockSpec((tm, D_MODEL), lambda i: (i, COL_GA * LANES // D_MODEL)),
            pl.BlockSpec((tm, D_MODEL), lambda i: (i, COL_GB * LANES // D_MODEL)),
            pl.BlockSpec((tm, D_MODEL), lambda i: (i, 0)),
            const((W, D_MODEL)), const((W, D_MODEL)), const((D_MODEL, D_MODEL)),
            const((1, D_MODEL)),
        ],
        out_specs=[
            pl.BlockSpec((tm, D_MODEL), lambda i: (i, 0)),
            pl.BlockSpec((tm, D_MODEL), lambda i: (i, 0)),
        ],
        out_shape=[
            jax.ShapeDtypeStruct((S, D_MODEL), F32),
            jax.ShapeDtypeStruct((S, D_MODEL), BF16),
        ],
        compiler_params=pltpu.CompilerParams(
            dimension_semantics=("parallel",), vmem_limit_bytes=VMEM_LIMIT),
        name="merge",
    )(ya, ob, z, z, x, wa, wb, wo, g2)


def _top_vals(vals, n):
    tops = []
    for _ in range(n):
        m = jnp.max(vals, axis=0, keepdims=True)
        tops.append(m)
        vals = jnp.where(vals == m, NEG_BIG, vals)
    return tops


def _peer_route_kernel(h2t_ref, wqt_ref, keys_ref, s2_ref, e2z_ref, th_ref, e1_ref):
    qT = _dot(wqt_ref[...], h2t_ref[...])
    ntop = PEER_TOPK + 1
    for h in range(HEADS):
        sc = []
        for half in range(2):
            j = 2 * h + half
            qj = qT[j * HEAD_DIM:(j + 1) * HEAD_DIM, :]
            q1, q2, _ = _split3(qj)
            kf = keys_ref[j]
            k1, k2, _ = _split3(kf)
            sc.append(_dot(k1, q1) + _dot(k1, q2) + _dot(k2, q1))
        s1, s2 = sc
        t1 = _top_vals(s1, ntop)
        t2 = _top_vals(s2, ntop)
        cand = [t1[a] + t2[b] for a in range(ntop) for b in range(ntop) if (a + 1) * (b + 1) <= ntop]
        cand = jnp.concatenate(cand, axis=0)
        ctop = _top_vals(cand, ntop)
        tau = 0.5 * (ctop[PEER_TOPK - 1] + ctop[PEER_TOPK])
        cmax = t1[0] + t2[0]
        zsum = jnp.sum(jnp.where(cand >= tau, jnp.exp(cand - cmax), 0.0), axis=0, keepdims=True)
        rows = slice(h * PEER_NKEYS, (h + 1) * PEER_NKEYS)
        s2_ref[rows, :] = s2
        e2z_ref[rows, :] = jnp.exp(s2 - t2[0]) / zsum
        th_ref[:, h, :] = tau - s1
        e1_ref[:, h, :] = jnp.exp(s1 - t1[0])


def _peer_route(h2t, wqt, keys, tb):
    S = h2t.shape[1]
    R = HEADS * PEER_NKEYS
    out = jax.ShapeDtypeStruct((R, S), F32)
    ospec = pl.BlockSpec((R, tb), lambda i: (0, i))
    out_k = jax.ShapeDtypeStruct((PEER_NKEYS, HEADS, S), F32)
    ospec_k = pl.BlockSpec((PEER_NKEYS, HEADS, tb), lambda i: (0, 0, i))
    return pl.pallas_call(
        _peer_route_kernel,
        grid=(S // tb,),
        in_specs=[
            pl.BlockSpec((D_MODEL, tb), lambda i: (0, i)),
            pl.BlockSpec((2 * R, D_MODEL), lambda i: (0, 0), pipeline_mode=pl.Buffered(1)),
            pl.BlockSpec((2 * HEADS, PEER_NKEYS, HEAD_DIM), lambda i: (0, 0, 0)),
        ],
        out_specs=[ospec, ospec, ospec_k, ospec_k],
        out_shape=[out, out, out_k, out_k],
        compiler_params=pltpu.CompilerParams(
            dimension_semantics=("parallel",), vmem_limit_bytes=VMEM_LIMIT),
        name="peer_route",
    )(h2t, wqt, keys)


def _gelu_tanh(a):
    return 0.5 * a * (1.0 + jnp.tanh(math.sqrt(2.0 / math.pi) * (a + 0.044715 * (a * a * a))))


def _peer_dense_kernel(h2t_ref, u_ref, vtp_ref, vtc_ref, s2_ref, e2z_ref, th_ref, e1_ref, o_ref,
                       y0, y1, *, te, tb, tm, tn, n_pairs):
    e = pl.program_id(1)

    @pl.when(e == 0)
    def _():
        o_ref[...] = jnp.zeros(o_ref.shape, F32)
        y1[...] = jnp.zeros(y1.shape, BF16)

    n_units = (te // tm) * (tb // tn)
    out_rows = D_MODEL * (tb // tn) // n_units

    def half(tile, y_cur, y_prev, vt_prev):
        for unit in range(n_units):
            n, m = divmod(unit, te // tm)
            cols = slice(n * tn, (n + 1) * tn)
            u_rows = slice(tile * te + m * tm, tile * te + (m + 1) * tm)
            hid = _dot(u_ref[u_rows, :], h2t_ref[:, cols])
            orow = slice(m * out_rows, (m + 1) * out_rows)
            o_ref[orow, cols] += _dot(vt_prev[orow, :], y_prev[:, cols])
            for cc in range(tm // PEER_NKEYS):
                i1 = m * (tm // PEER_NKEYS) + cc
                k1 = tile * (te // PEER_NKEYS) + i1
                for tc in range(tn // LANES):
                    lanes = slice(n * tn + tc * LANES, n * tn + (tc + 1) * LANES)
                    w = jnp.zeros((PEER_NKEYS, LANES), F32)
                    for h in range(HEADS):
                        rows = slice(h * PEER_NKEYS, (h + 1) * PEER_NKEYS)
                        th_row = th_ref[k1, h:h + 1, lanes]
                        e1_row = e1_ref[k1, h:h + 1, lanes]
                        w = w + jnp.where(s2_ref[rows, lanes] >= th_row,
                                          e2z_ref[rows, lanes], 0.0) * e1_row
                    act = _gelu_tanh(hid[cc * PEER_NKEYS:(cc + 1) * PEER_NKEYS,
                                         tc * LANES:(tc + 1) * LANES])
                    y_cur[i1 * PEER_NKEYS:(i1 + 1) * PEER_NKEYS, lanes] = (w * act).astype(BF16)

    half(0, y0, y1, vtp_ref)

    @pl.when(e < n_pairs)
    def _():
        half(1, y1, y0, vtc_ref)


def _peer_dense(h2t, u, vt, s2, e2z, th, e1, tb, te, tm, tn):
    S = h2t.shape[1]
    n_pairs = u.shape[0] // (2 * te)
    R = HEADS * PEER_NKEYS
    rspec = pl.BlockSpec((R, tb), lambda i, e: (0, i))
    last = n_pairs - 1
    kspec = pl.BlockSpec((2 * te // PEER_NKEYS, HEADS, tb), lambda i, e: (jnp.minimum(e, last), 0, i))
    return pl.pallas_call(
        functools.partial(_peer_dense_kernel, te=te, tb=tb, tm=tm, tn=tn, n_pairs=n_pairs),
        grid=(S // tb, n_pairs + 1),
        in_specs=[
            pl.BlockSpec((D_MODEL, tb), lambda i, e: (0, i)),
            pl.BlockSpec((2 * te, D_MODEL), lambda i, e: (jnp.minimum(e, last), 0)),
            pl.BlockSpec((D_MODEL, te), lambda i, e: (0, jnp.maximum(2 * e - 1, 0))),
            pl.BlockSpec((D_MODEL, te), lambda i, e: (0, jnp.minimum(2 * e, 2 * last))),
            rspec, rspec, kspec, kspec,
        ],
        out_specs=pl.BlockSpec((D_MODEL, tb), lambda i, e: (0, i)),
        out_shape=jax.ShapeDtypeStruct((D_MODEL, S), F32),
        scratch_shapes=[pltpu.VMEM((te, tb), BF16), pltpu.VMEM((te, tb), BF16)],
        compiler_params=pltpu.CompilerParams(
            dimension_semantics=("parallel", "arbitrary"), vmem_limit_bytes=VMEM_LIMIT),
        name="peer_dense",
    )(h2t, u, vt, vt, s2, e2z, th, e1)


def _final_kernel(x1_ref, pt_ref, g_ref, o_ref):
    xo = x1_ref[...] + pt_ref[...].T
    ms = jnp.mean(xo * xo, axis=-1, keepdims=True)
    o_ref[...] = xo * lax.rsqrt(ms + EPS) * g_ref[...]


def _final(x1, peer_t, g, tm):
    S = x1.shape[0]
    return pl.pallas_call(
        _final_kernel,
        grid=(S // tm,),
        in_specs=[
            pl.BlockSpec((tm, D_MODEL), lambda i: (i, 0)),
            pl.BlockSpec((D_MODEL, tm), lambda i: (0, i)),
            pl.BlockSpec((1, D_MODEL), lambda i: (0, 0)),
        ],
        out_specs=pl.BlockSpec((tm, D_MODEL), lambda i: (i, 0)),
        out_shape=jax.ShapeDtypeStruct((S, D_MODEL), F32),
        compiler_params=pltpu.CompilerParams(
            dimension_semantics=("parallel",), vmem_limit_bytes=VMEM_LIMIT),
        name="final_norm",
    )(x1, peer_t, g)


def _tile(n, pref):
    t = min(pref, n)
    assert n % t == 0, (n, t)
    return t


def kernel(x, norm1_g, w_in, fox_f_bias, hg_lb_logits, hg_norm_g, w_a_up, w_b_up, w_o, norm2_g,
           peer_wq, peer_keys, peer_u, peer_v, norm_f_g):
    B, S, D = x.shape
    assert B == 1 and D == D_MODEL and w_in.shape[0] == 1 and S % LANES == 0
    xs = x[0]
    W = HEADS * HEAD_DIM

    wi = w_in[0]
    o_ff = 3 * W
    o_rest = o_ff + HEADS
    o_ga = o_rest + 4 * W
    w_main = jnp.concatenate(
        [wi[:, o_ga:], wi[:, :o_ff], wi[:, o_rest:o_ga]], axis=1).astype(BF16)
    w_ffT = wi[:, o_ff:o_rest].T
    wa = w_a_up[0].astype(BF16)
    wb = w_b_up[0].astype(BF16)
    wo = w_o[0].astype(BF16)
    wqt = peer_wq[0].T.astype(BF16)
    keys = peer_keys[0].reshape(2 * HEADS, PEER_NKEYS, HEAD_DIM)
    u = peer_u[0].astype(BF16)
    vt = peer_v[0].T.astype(BF16)

    z, ffT = _inproj(xs, norm1_g, w_main, w_ffT, _tile(S, 1024), 512)

    nb = S // LANES
    bias_rows = jnp.broadcast_to(fox_f_bias[0][:, None, None], (HEADS, nb, LANES))
    c_rows = _fox_gate(ffT.reshape(HEADS * nb, LANES), bias_rows.reshape(HEADS * nb, LANES), nb)
    c2 = c_rows.reshape(HEADS, S)
    ya = _fox_attn(z, c2, _fox_skip_table(_fox_stats(z), c2))

    ob = _hgrn2(z, hg_lb_logits, hg_norm_g, _tile(S, 256))

    x1, h2 = _merge(ya, ob, z, xs, wa, wb, wo, norm2_g, _tile(S, 256))

    h2t = h2.T
    s2, e2z, th, e1 = _peer_route(h2t, wqt, keys, _tile(S, 256))
    peer_t = _peer_dense(h2t, u, vt, s2, e2z, th, e1, _tile(S, 512), 512, 256, 256)

    out = _final(x1, peer_t, norm_f_g.reshape(1, D_MODEL), _tile(S, 256))
    return out[None]
```

```python
import functools
import math

import jax
import jax.numpy as jnp
from jax import lax
from jax.experimental import pallas as pl
from jax.experimental.pallas import tpu as pltpu

F32 = jnp.float32
BF16 = jnp.bfloat16

D_MODEL = 2048
HEADS = 8
HEAD_DIM = 128
PEER_NKEYS = 128
PEER_TOPK = 16
EPS = 1e-6
LANES = 128
HG_CHUNK = 64
HG_SUB = 16
VMEM_LIMIT = 56 * 1024 * 1024
NEG_BIG = -1e30
LOG2E = math.log2(math.e)
FOX_BLK = 512
FOX_SKIP_GAP = 130.0

COL_FQ, COL_FK, COL_FV = 0, 8, 16
N_A = 24 * LANES
COL_HQ, COL_HF, COL_HI, COL_HGATE, COL_GA, COL_GB = 0, 8, 16, 24, 32, 48
N_B = 64 * LANES

NT_DIMS = (((1,), (1,)), ((), ()))


def _split3(a):
    a1 = a.astype(BF16)
    r1 = a - a1.astype(F32)
    a2 = r1.astype(BF16)
    a3 = (r1 - a2.astype(F32)).astype(BF16)
    return a1, a2, a3


def _dot(a, b):
    return jnp.dot(a, b, preferred_element_type=F32)


def _dot_nt(a, b):
    return lax.dot_general(a, b, NT_DIMS, preferred_element_type=F32)


def _sigmoid(a):
    return 1.0 / (1.0 + jnp.exp(-a))


def _inproj_a_kernel(x_ref, g_ref, w_ref, wff_ref, z_ref, ff_ref, h_ref):
    @pl.when(pl.program_id(1) == 0)
    def _():
        x = x_ref[...]
        ms = jnp.mean(x * x, axis=-1, keepdims=True)
        h = x * lax.rsqrt(ms + EPS) * g_ref[...]
        hb = h.astype(BF16)
        h_ref[...] = hb
        hl = (h - hb.astype(F32)).astype(BF16)
        wf = wff_ref[...]
        wb = wf.astype(BF16)
        wl = (wf - wb.astype(F32)).astype(BF16)
        ff_ref[...] = _dot_nt(wb, hb) + _dot_nt(wb, hl) + _dot_nt(wl, hb)

    z_ref[...] = _dot(h_ref[...], w_ref[...]).astype(z_ref.dtype)


def _inproj_a(x, g, w_a, w_ffT, tm, tn):
    S = x.shape[0]
    return pl.pallas_call(
        _inproj_a_kernel,
        grid=(S // tm, N_A // tn),
        in_specs=[
            pl.BlockSpec((tm, D_MODEL), lambda i, j: (i, 0)),
            pl.BlockSpec((1, D_MODEL), lambda i, j: (0, 0)),
            pl.BlockSpec((D_MODEL, tn), lambda i, j: (0, j)),
            pl.BlockSpec((HEADS, D_MODEL), lambda i, j: (0, 0)),
        ],
        out_specs=[
            pl.BlockSpec((tm, tn), lambda i, j: (i, j)),
            pl.BlockSpec((HEADS, tm), lambda i, j: (0, i)),
            pl.BlockSpec((tm, D_MODEL), lambda i, j: (i, 0)),
        ],
        out_shape=[
            jax.ShapeDtypeStruct((S, N_A), BF16),
            jax.ShapeDtypeStruct((HEADS, S), F32),
            jax.ShapeDtypeStruct((S, D_MODEL), BF16),
        ],
        compiler_params=pltpu.CompilerParams(
            dimension_semantics=("parallel", "arbitrary"),
            vmem_limit_bytes=VMEM_LIMIT),
        name="inproj_a",
    )(x, g, w_a, w_ffT)


def _inproj_b_kernel(h_ref, w_ref, z_ref):
    z_ref[...] = _dot(h_ref[...], w_ref[...]).astype(z_ref.dtype)


def _inproj_b(h, w_b, tm, tn):
    S = h.shape[0]
    return pl.pallas_call(
        _inproj_b_kernel,
        grid=(S // tm, N_B // tn),
        in_specs=[
            pl.BlockSpec((tm, D_MODEL), lambda i, j: (i, 0)),
            pl.BlockSpec((D_MODEL, tn), lambda i, j: (0, j)),
        ],
        out_specs=pl.BlockSpec((tm, tn), lambda i, j: (i, j)),
        out_shape=jax.ShapeDtypeStruct((S, N_B), BF16),
        compiler_params=pltpu.CompilerParams(
            dimension_semantics=("parallel", "arbitrary"),
            vmem_limit_bytes=VMEM_LIMIT),
        name="inproj_b",
    )(h, w_b)


def _fox_gate_kernel(ff_ref, bias_ref, c_ref, *, nb):
    a = ff_ref[...] + bias_ref[...]
    lf = jnp.minimum(a, 0.0) - jnp.log(1.0 + jnp.exp(-jnp.abs(a)))
    R = HEADS * nb
    r = lax.broadcasted_iota(jnp.int32, (LANES, LANES), 0)
    c = lax.broadcasted_iota(jnp.int32, (LANES, LANES), 1)
    upper = jnp.where(r <= c, 1.0, 0.0).astype(BF16)
    ones = jnp.ones((LANES, LANES), BF16)
    rr = lax.broadcasted_iota(jnp.int32, (R, R), 0)
    cc = lax.broadcasted_iota(jnp.int32, (R, R), 1)
    same_head = (rr // nb) == (cc // nb)
    carry_m = jnp.where(same_head & (cc < rr), 1.0, 0.0).astype(BF16)
    l1, l2, l3 = _split3(lf)
    within = _dot(l1, upper) + _dot(l2, upper) + _dot(l3, upper)
    tot = _dot(l1, ones) + _dot(l2, ones) + _dot(l3, ones)
    t1, t2, t3 = _split3(tot)
    carry = _dot(carry_m, t1) + _dot(carry_m, t2) + _dot(carry_m, t3)
    c_ref[...] = (within + carry) * LOG2E


def _fox_gate(ff_rows, bias_rows, nb):
    R = HEADS * nb
    return pl.pallas_call(
        functools.partial(_fox_gate_kernel, nb=nb),
        out_shape=jax.ShapeDtypeStruct((R, LANES), F32),
        compiler_params=pltpu.CompilerParams(vmem_limit_bytes=VMEM_LIMIT),
        name="fox_gate",
    )(ff_rows, bias_rows)


def _fox_stats_kernel(qk_ref, o_ref, *, nblk):
    x = qk_ref[...].astype(F32)
    x2 = (x * x).astype(BF16)
    r = lax.broadcasted_iota(jnp.int32, (2 * HEADS * HEAD_DIM, LANES), 0)
    c = lax.broadcasted_iota(jnp.int32, (2 * HEADS * HEAD_DIM, LANES), 1)
    grp = jnp.where(r // HEAD_DIM == c, 1.0, 0.0).astype(BF16)
    n2 = _dot(x2, grp)
    o_ref[...] = jnp.max(n2.reshape(nblk, FOX_BLK, LANES), axis=1)


def _fox_stats(z):
    S = z.shape[0]
    rows = min(S, 2 * FOX_BLK)
    assert S % rows == 0 and (COL_FK - COL_FQ) == HEADS and COL_FQ % (2 * HEADS) == 0
    nblk = rows // FOX_BLK
    st = pl.pallas_call(
        functools.partial(_fox_stats_kernel, nblk=nblk),
        grid=(S // rows,),
        in_specs=[pl.BlockSpec((rows, 2 * HEADS * HEAD_DIM), lambda i: (i, COL_FQ // (2 * HEADS)))],
        out_specs=pl.BlockSpec((None, nblk, LANES), lambda i: (i, 0, 0)),
        out_shape=jax.ShapeDtypeStruct((S // rows, nblk, LANES), F32),
        compiler_params=pltpu.CompilerParams(
            dimension_semantics=("parallel",), vmem_limit_bytes=VMEM_LIMIT),
        name="fox_stats",
    )(z)
    return st.reshape(S // FOX_BLK, LANES)


def _fox_attn_kernel(skip_ref, q_ref, k_ref, v_ref, c_ref, o_ref, vt_s, ck_s, m_s, l_s, acc, *, nq):
    h = pl.program_id(0)
    i = pl.program_id(1)
    B = FOX_BLK
    n128 = B // LANES

    @pl.when(i == 0)
    def _():
        for b in range(nq * n128):
            j, c = divmod(b, n128)
            rows = slice(b * LANES, (b + 1) * LANES)
            vt_s[j, :, c * LANES:(c + 1) * LANES] = v_ref[rows, :].astype(F32).T.astype(BF16)
            ck_s[rows, :] = jnp.broadcast_to(c_ref[b:b + 1, :], (LANES, LANES)).T

    qt = (q_ref[...].astype(F32) * (HEAD_DIM ** -0.5 * LOG2E)).T.astype(BF16)
    m_s[...] = jnp.full(m_s.shape, NEG_BIG, F32)
    l_s[...] = jnp.zeros(l_s.shape, F32)
    acc[...] = jnp.zeros(acc.shape, F32)

    def scores(rows):
        return _dot(k_ref[rows, :], qt) - pltpu.repeat(ck_s[rows, :], n128, axis=1)

    def update(st, vt_blk):
        m_prev = m_s[...]
        m_new = jnp.maximum(m_prev, jnp.max(st, axis=0, keepdims=True))
        alpha = jnp.exp2(m_prev - m_new)
        p = jnp.exp2(st - m_new)
        l_s[...] = alpha * l_s[...] + jnp.sum(p, axis=0, keepdims=True)
        acc[...] = alpha * acc[...] + _dot(vt_blk, p.astype(BF16))
        m_s[...] = m_new

    def body(j, carry):
        @pl.when(skip_ref[(h * nq + i) * nq + j] == 0)
        def _():
            update(scores(pl.ds(pl.multiple_of(j * B, B), B)), vt_s[j])
        return carry

    lax.fori_loop(0, i, body, 0)

    st = scores(pl.ds(pl.multiple_of(i * B, B), B))
    key = lax.broadcasted_iota(jnp.int32, (B, B), 0)
    qry = lax.broadcasted_iota(jnp.int32, (B, B), 1)
    update(jnp.where(key <= qry, st, NEG_BIG), vt_s[i])
    o_ref[...] = (acc[...] / l_s[...]).T.astype(o_ref.dtype)


def _fox_attn(z, c_rows, skip):
    S = z.shape[0]
    B = FOX_BLK
    nq = S // B
    grid_spec = pltpu.PrefetchScalarGridSpec(
        num_scalar_prefetch=1,
        grid=(HEADS, nq),
        in_specs=[
            pl.BlockSpec((B, HEAD_DIM), lambda h, i, sk: (i, COL_FQ + h)),
            pl.BlockSpec((S, HEAD_DIM), lambda h, i, sk: (0, COL_FK + h)),
            pl.BlockSpec((S, HEAD_DIM), lambda h, i, sk: (0, COL_FV + h)),
            pl.BlockSpec((None, S // LANES, LANES), lambda h, i, sk: (h, 0, 0)),
        ],
        out_specs=pl.BlockSpec((B, HEAD_DIM), lambda h, i, sk: (i, h)),
        scratch_shapes=[
            pltpu.VMEM((nq, HEAD_DIM, B), BF16),
            pltpu.VMEM((S, LANES), F32),
            pltpu.VMEM((1, B), F32),
            pltpu.VMEM((1, B), F32),
            pltpu.VMEM((HEAD_DIM, B), F32),
        ],
    )
    return pl.pallas_call(
        functools.partial(_fox_attn_kernel, nq=nq),
        grid_spec=grid_spec,
        out_shape=jax.ShapeDtypeStruct((S, HEADS * HEAD_DIM), BF16),
        compiler_params=pltpu.CompilerParams(
            dimension_semantics=("parallel", "arbitrary"),
            vmem_limit_bytes=VMEM_LIMIT),
        name="fox_attn",
    )(skip, z, z, z, c_rows)


def _fox_skip_table(stats, c2):
    B = FOX_BLK
    qn = jnp.sqrt(stats[:, 0:HEADS] * 1.02).T
    kn = jnp.sqrt(stats[:, HEADS:2 * HEADS] * 1.02).T
    cf = c2[:, 0::B]
    cl = c2[:, B - 1::B]
    sc = HEAD_DIM ** -0.5 * LOG2E
    bound = (sc * qn[:, :, None] * (kn[:, None, :] + kn[:, :, None])
             + cf[:, :, None] - cl[:, None, :])
    nq = qn.shape[1]
    below = jnp.arange(nq)[None, :] < jnp.arange(nq)[:, None]
    return ((bound < -FOX_SKIP_GAP) & below[None]).astype(jnp.int32).reshape(-1)


def _hgrn2_kernel(hq_ref, hf_ref, hi_ref, hg_ref, lbl_ref, ng_ref, o_ref, st, *, n_chunks):
    @pl.when(pl.program_id(0) == 0)
    def _():
        st[...] = jnp.zeros(st.shape, F32)

    lbl = lbl_ref[...]
    mx = jnp.max(lbl, axis=0, keepdims=True)
    ex = jnp.exp(lbl - mx)
    lb = ex[0:1, :] / jnp.sum(ex, axis=0, keepdims=True)
    ng = ng_ref[...]

    C = HG_CHUNK
    r = lax.broadcasted_iota(jnp.int32, (C, C), 0)
    c = lax.broadcasted_iota(jnp.int32, (C, C), 1)
    causal = c <= r
    lower = jnp.where(causal, 1.0, 0.0).astype(BF16)
    n_sub = C // HG_SUB

    def chunk(ci, carry):
        rows = pl.ds(pl.multiple_of(ci * C, C), C)
        f = lb + (1.0 - lb) * _sigmoid(hf_ref[rows, :].astype(F32))
        g = jnp.log(f)
        kk = 1.0 - f
        hq = hq_ref[rows, :].astype(F32)
        q = hq * _sigmoid(hq)
        vb = hi_ref[rows, :]
        g1, g2, g3 = _split3(g)
        b = _dot(lower, g1) + _dot(lower, g2) + _dot(lower, g3)
        qe = (q * jnp.exp(b)).astype(BF16)
        bl = b[C - 1:C, :]
        kd = (kk * jnp.exp(bl - b)).astype(BF16)
        dec = jnp.exp(bl)
        qis, kis = [], []
        for s in range(n_sub):
            r0 = s * HG_SUB
            b0 = b[r0 - 1:r0, :] if s > 0 else jnp.zeros((1, b.shape[1]), F32)
            qis.append((q[r0:r0 + HG_SUB, :] * jnp.exp(b[r0:r0 + HG_SUB, :] - b0)).astype(BF16))
            kis.append((kk * jnp.exp(b0 - b)).astype(BF16))
        heads = [slice(h * HEAD_DIM, (h + 1) * HEAD_DIM) for h in range(HEADS)]
        stv = [st[h] for h in range(HEADS)]
        o_int = [_dot_nt(qe[:, cs], stv[h].astype(BF16)) for h, cs in enumerate(heads)]
        att = [jnp.concatenate([_dot_nt(qis[s][:, cs], kis[s][:, cs]) for s in range(n_sub)], axis=0)
               for cs in heads]
        st_new = [_dot(vb[:, cs].astype(F32).T.astype(BF16), kd[:, cs]) for cs in heads]
        att = [jnp.where(causal, a, 0.0).astype(BF16) for a in att]
        o_heads = []
        for h, cs in enumerate(heads):
            o = o_int[h] + _dot(att[h], vb[:, cs])
            st[h] = stv[h] * dec[:, cs] + st_new[h]
            o_heads.append(o * lax.rsqrt(jnp.mean(o * o, axis=-1, keepdims=True) + EPS))
        on = jnp.concatenate(o_heads, axis=1) * ng
        hg = hg_ref[rows, :].astype(F32)
        o_ref[rows, :] = (on * (hg * _sigmoid(hg))).astype(o_ref.dtype)
        return carry

    lax.fori_loop(0, n_chunks, chunk, 0)


def _hgrn2(z, lb_logits, norm_g, tc):
    S = z.shape[0]
    W = HEADS * HEAD_DIM
    blk = lambda off: pl.BlockSpec((tc, W), lambda t: (t, off // HEADS))
    assert all(o % HEADS == 0 for o in (COL_HQ, COL_HF, COL_HI, COL_HGATE))
    return pl.pallas_call(
        functools.partial(_hgrn2_kernel, n_chunks=tc // HG_CHUNK),
        grid=(S // tc,),
        in_specs=[
            blk(COL_HQ), blk(COL_HF), blk(COL_HI), blk(COL_HGATE),
            pl.BlockSpec((lb_logits.shape[0], W), lambda t: (0, 0)),
            pl.BlockSpec((1, W), lambda t: (0, 0)),
        ],
        out_specs=pl.BlockSpec((tc, W), lambda t: (t, 0)),
        out_shape=jax.ShapeDtypeStruct((S, W), BF16),
        scratch_shapes=[pltpu.VMEM((HEADS, HEAD_DIM, HEAD_DIM), F32)],
        compiler_params=pltpu.CompilerParams(
            dimension_semantics=("arbitrary",),
            vmem_limit_bytes=VMEM_LIMIT),
        name="hgrn2",
    )(z, z, z, z, lb_logits, norm_g)


def _merge_kernel(ya_ref, ob_ref, ga_ref, gb_ref, x_ref, wa_ref, wb_ref, wo_ref, g2_ref,
                  x1_ref, h2t_ref):
    ya = _dot(ya_ref[...], wa_ref[...])
    yb = _dot(ob_ref[...], wb_ref[...])
    merged = _sigmoid(ga_ref[...].astype(F32)) * ya + _sigmoid(gb_ref[...].astype(F32)) * yb
    x1 = x_ref[...] + _dot(merged.astype(BF16), wo_ref[...])
    x1_ref[...] = x1
    ms = jnp.mean(x1 * x1, axis=-1, keepdims=True)
    h2t_ref[...] = (x1 * lax.rsqrt(ms + EPS) * g2_ref[...]).T.astype(h2t_ref.dtype)


def _merge(ya, ob, z, x, wa, wb, wo, g2, tm):
    S = x.shape[0]
    W = HEADS * HEAD_DIM
    const = lambda shape: pl.BlockSpec(shape, lambda i: (0, 0), pipeline_mode=pl.Buffered(1))
    return pl.pallas_call(
        _merge_kernel,
        grid=(S // tm,),
        in_specs=[
            pl.BlockSpec((tm, W), lambda i: (i, 0)),
            pl.BlockSpec((tm, W), lambda i: (i, 0)),
            pl.BlockSpec((tm, D_MODEL), lambda i: (i, COL_GA * LANES // D_MODEL)),
            pl.BlockSpec((tm, D_MODEL), lambda i: (i, COL_GB * LANES // D_MODEL)),
            pl.BlockSpec((tm, D_MODEL), lambda i: (i, 0)),
            const((W, D_MODEL)), const((W, D_MODEL)), const((D_MODEL, D_MODEL)),
            const((1, D_MODEL)),
        ],
        out_specs=[
            pl.BlockSpec((tm, D_MODEL), lambda i: (i, 0)),
            pl.BlockSpec((D_MODEL, tm), lambda i: (0, i)),
        ],
        out_shape=[
            jax.ShapeDtypeStruct((S, D_MODEL), F32),
            jax.ShapeDtypeStruct((D_MODEL, S), BF16),
        ],
        compiler_params=pltpu.CompilerParams(
            dimension_semantics=("parallel",), vmem_limit_bytes=VMEM_LIMIT),
        name="merge",
    )(ya, ob, z, z, x, wa, wb, wo, g2)


def _top_vals(vals, n):
    tops = []
    for _ in range(n):
        m = jnp.max(vals, axis=0, keepdims=True)
        tops.append(m)
        vals = jnp.where(vals == m, NEG_BIG, vals)
    return tops


def _oddeven_merge(lo, hi, r):
    step = r * 2
    if step < hi - lo:
        yield from _oddeven_merge(lo, hi, step)
        yield from _oddeven_merge(lo + r, hi, step)
        yield from [(i, i + r) for i in range(lo + r, hi - r, step)]
    else:
        yield (lo, lo + r)


def _oddeven_merge_sort(lo, hi):
    if hi - lo >= 1:
        mid = lo + (hi - lo) // 2
        yield from _oddeven_merge_sort(lo, mid)
        yield from _oddeven_merge_sort(mid + 1, hi)
        yield from _oddeven_merge(lo, hi, 1)


def _top_vals_sorted_groups(vals, n):
    SUB = 8
    nv = vals.shape[0] // SUB
    v = [vals[SUB * k:SUB * (k + 1), :] for k in range(nv)]
    for a, b in _oddeven_merge_sort(0, nv - 1):
        v[a], v[b] = jnp.maximum(v[a], v[b]), jnp.minimum(v[a], v[b])
    tops = []
    for r in range(n):
        m = jnp.max(v[0], axis=0, keepdims=True)
        tops.append(m)
        if r == n - 1:
            break
        hit = v[0] == m
        depth = min(nv, n - 1 - r)
        for k in range(depth):
            nxt = v[k + 1] if k + 1 < nv else NEG_BIG
            v[k] = jnp.where(hit, nxt, v[k])
    return tops


def _peer_route_kernel(h2t_ref, wqt_ref, keys_ref, s2_ref, e2z_ref, th_ref, e1_ref):
    qT = _dot(wqt_ref[...], h2t_ref[...])
    ntop = PEER_TOPK + 1
    for h in range(HEADS):
        sc = []
        for half in range(2):
            j = 2 * h + half
            qj = qT[j * HEAD_DIM:(j + 1) * HEAD_DIM, :]
            q1, q2, _ = _split3(qj)
            kf = keys_ref[j]
            k1, k2, _ = _split3(kf)
            sc.append(_dot(k1, q1) + _dot(k1, q2) + _dot(k2, q1))
        s1, s2 = sc
        t1 = _top_vals_sorted_groups(s1, ntop)
        t2 = _top_vals_sorted_groups(s2, ntop)
        cand = [t1[a] + t2[b] for a in range(ntop) for b in range(ntop) if (a + 1) * (b + 1) <= ntop]
        cand = jnp.concatenate(cand, axis=0)
        ctop = _top_vals(cand, ntop)
        tau = 0.5 * (ctop[PEER_TOPK - 1] + ctop[PEER_TOPK])
        cmax = t1[0] + t2[0]
        zsum = jnp.sum(jnp.where(cand >= tau, jnp.exp(cand - cmax), 0.0), axis=0, keepdims=True)
        rows = slice(h * PEER_NKEYS, (h + 1) * PEER_NKEYS)
        s2_ref[rows, :] = s2
        e2z_ref[rows, :] = jnp.exp(s2 - t2[0]) / zsum
        th_ref[:, h, :] = tau - s1
        e1_ref[:, h, :] = jnp.exp(s1 - t1[0])


def _peer_route(h2t, wqt, keys, tb):
    S = h2t.shape[1]
    R = HEADS * PEER_NKEYS
    out = jax.ShapeDtypeStruct((R, S), F32)
    ospec = pl.BlockSpec((R, tb), lambda i: (0, i))
    out_k = jax.ShapeDtypeStruct((PEER_NKEYS, HEADS, S), F32)
    ospec_k = pl.BlockSpec((PEER_NKEYS, HEADS, tb), lambda i: (0, 0, i))
    return pl.pallas_call(
        _peer_route_kernel,
        grid=(S // tb,),
        in_specs=[
            pl.BlockSpec((D_MODEL, tb), lambda i: (0, i)),
            pl.BlockSpec((2 * R, D_MODEL), lambda i: (0, 0), pipeline_mode=pl.Buffered(1)),
            pl.BlockSpec((2 * HEADS, PEER_NKEYS, HEAD_DIM), lambda i: (0, 0, 0)),
        ],
        out_specs=[ospec, ospec, ospec_k, ospec_k],
        out_shape=[out, out, out_k, out_k],
        compiler_params=pltpu.CompilerParams(
            dimension_semantics=("parallel",), vmem_limit_bytes=VMEM_LIMIT),
        name="peer_route",
    )(h2t, wqt, keys)


def _gelu_tanh(a):
    return 0.5 * a * (1.0 + jnp.tanh(math.sqrt(2.0 / math.pi) * (a + 0.044715 * (a * a * a))))


def _peer_dense_kernel(h2t_ref, u_ref, vtp_ref, vtc_ref, s2_ref, e2z_ref, th_ref, e1_ref, o_ref,
                       y0, y1, *, te, tb, tm, tn, n_pairs):
    e = pl.program_id(1)

    @pl.when(e == 0)
    def _():
        o_ref[...] = jnp.zeros(o_ref.shape, F32)
        y1[...] = jnp.zeros(y1.shape, BF16)

    n_units = (te // tm) * (tb // tn)
    out_rows = D_MODEL * (tb // tn) // n_units

    def half(tile, y_cur, y_prev, vt_prev):
        for unit in range(n_units):
            n, m = divmod(unit, te // tm)
            cols = slice(n * tn, (n + 1) * tn)
            u_rows = slice(tile * te + m * tm, tile * te + (m + 1) * tm)
            hid = _dot(u_ref[u_rows, :], h2t_ref[:, cols])
            orow = slice(m * out_rows, (m + 1) * out_rows)
            o_ref[orow, cols] += _dot(vt_prev[orow, :], y_prev[:, cols])
            for cc in range(tm // PEER_NKEYS):
                i1 = m * (tm // PEER_NKEYS) + cc
                k1 = tile * (te // PEER_NKEYS) + i1
                for tc in range(tn // LANES):
                    lanes = slice(n * tn + tc * LANES, n * tn + (tc + 1) * LANES)
                    w = jnp.zeros((PEER_NKEYS, LANES), F32)
                    for h in range(HEADS):
                        rows = slice(h * PEER_NKEYS, (h + 1) * PEER_NKEYS)
                        th_row = th_ref[k1, h:h + 1, lanes]
                        e1_row = e1_ref[k1, h:h + 1, lanes]
                        w = w + jnp.where(s2_ref[rows, lanes] >= th_row,
                                          e2z_ref[rows, lanes], 0.0) * e1_row
                    act = _gelu_tanh(hid[cc * PEER_NKEYS:(cc + 1) * PEER_NKEYS,
                                         tc * LANES:(tc + 1) * LANES])
                    y_cur[i1 * PEER_NKEYS:(i1 + 1) * PEER_NKEYS, lanes] = (w * act).astype(BF16)

    half(0, y0, y1, vtp_ref)

    @pl.when(e < n_pairs)
    def _():
        half(1, y1, y0, vtc_ref)


def _peer_dense(h2t, u, vt, s2, e2z, th, e1, tb, te, tm, tn):
    S = h2t.shape[1]
    n_pairs = u.shape[0] // (2 * te)
    R = HEADS * PEER_NKEYS
    rspec = pl.BlockSpec((R, tb), lambda i, e: (0, i))
    last = n_pairs - 1
    kspec = pl.BlockSpec((2 * te // PEER_NKEYS, HEADS, tb), lambda i, e: (jnp.minimum(e, last), 0, i))
    return pl.pallas_call(
        functools.partial(_peer_dense_kernel, te=te, tb=tb, tm=tm, tn=tn, n_pairs=n_pairs),
        grid=(S // tb, n_pairs + 1),
        in_specs=[
            pl.BlockSpec((D_MODEL, tb), lambda i, e: (0, i)),
            pl.BlockSpec((2 * te, D_MODEL), lambda i, e: (jnp.minimum(e, last), 0)),
            pl.BlockSpec((D_MODEL, te), lambda i, e: (0, jnp.maximum(2 * e - 1, 0))),
            pl.BlockSpec((D_MODEL, te), lambda i, e: (0, jnp.minimum(2 * e, 2 * last))),
            rspec, rspec, kspec, kspec,
        ],
        out_specs=pl.BlockSpec((D_MODEL, tb), lambda i, e: (0, i)),
        out_shape=jax.ShapeDtypeStruct((D_MODEL, S), F32),
        scratch_shapes=[pltpu.VMEM((te, tb), BF16), pltpu.VMEM((te, tb), BF16)],
        compiler_params=pltpu.CompilerParams(
            dimension_semantics=("parallel", "arbitrary"), vmem_limit_bytes=VMEM_LIMIT),
        name="peer_dense",
    )(h2t, u, vt, vt, s2, e2z, th, e1)


def _final_kernel(x1_ref, pt_ref, g_ref, o_ref):
    xo = x1_ref[...] + pt_ref[...].T
    ms = jnp.mean(xo * xo, axis=-1, keepdims=True)
    o_ref[...] = xo * lax.rsqrt(ms + EPS) * g_ref[...]


def _final(x1, peer_t, g, tm):
    S = x1.shape[0]
    return pl.pallas_call(
        _final_kernel,
        grid=(S // tm,),
        in_specs=[
            pl.BlockSpec((tm, D_MODEL), lambda i: (i, 0)),
            pl.BlockSpec((D_MODEL, tm), lambda i: (0, i)),
            pl.BlockSpec((1, D_MODEL), lambda i: (0, 0)),
        ],
        out_specs=pl.BlockSpec((tm, D_MODEL), lambda i: (i, 0)),
        out_shape=jax.ShapeDtypeStruct((S, D_MODEL), F32),
        compiler_params=pltpu.CompilerParams(
            dimension_semantics=("parallel",), vmem_limit_bytes=VMEM_LIMIT),
        name="final_norm",
    )(x1, peer_t, g)


def _tile(n, pref):
    t = min(pref, n)
    assert n % t == 0, (n, t)
    return t


def kernel(x, norm1_g, w_in, fox_f_bias, hg_lb_logits, hg_norm_g, w_a_up, w_b_up, w_o, norm2_g,
           peer_wq, peer_keys, peer_u, peer_v, norm_f_g):
    B, S, D = x.shape
    assert B == 1 and D == D_MODEL and w_in.shape[0] == 1 and S % LANES == 0
    xs = x[0]
    W = HEADS * HEAD_DIM

    wi = w_in[0]
    o_ff = 3 * W
    o_rest = o_ff + HEADS
    w_a = wi.astype(BF16)
    w_b = w_a[:, o_rest:]
    w_ffT = wi[:, o_ff:o_rest].T
    wa = w_a_up[0].astype(BF16)
    wb = w_b_up[0].astype(BF16)
    wo = w_o[0].astype(BF16)
    wqt = peer_wq[0].T.astype(BF16)
    keys = peer_keys[0].reshape(2 * HEADS, PEER_NKEYS, HEAD_DIM)
    u = peer_u[0].astype(BF16)
    vt = peer_v[0].T.astype(BF16)

    za, ffT, h = _inproj_a(xs, norm1_g, w_a, w_ffT, _tile(S, 1024), 512)
    zb = _inproj_b(h, w_b, _tile(S, 1024), 512)

    nb = S // LANES
    bias_rows = jnp.broadcast_to(fox_f_bias[0][:, None, None], (HEADS, nb, LANES))
    c_rows = _fox_gate(ffT.reshape(HEADS * nb, LANES), bias_rows.reshape(HEADS * nb, LANES), nb)
    skip = _fox_skip_table(_fox_stats(za), c_rows.reshape(HEADS, S))
    ya = _fox_attn(za, c_rows.reshape(HEADS, nb, LANES), skip)

    ob = _hgrn2(zb, hg_lb_logits, hg_norm_g, _tile(S, 256))

    x1, h2t = _merge(ya, ob, zb, xs, wa, wb, wo, norm2_g, _tile(S, 256))

    s2, e2z, th, e1 = _peer_route(h2t, wqt, keys, _tile(S, 256))
    peer_t = _peer_dense(h2t, u, vt, s2, e2z, th, e1, _tile(S, 512), 512, 256, 256)

    out = _final(x1, peer_t, norm_f_g.reshape(1, D_MODEL), _tile(S, 256))
    return out[None]
```
